```python
import jax, jax.numpy as jnp
from jax import lax
import numpy as np

D_MODEL = 1024
BATCH = 2
SEQ = 8192
DEPTH = 4

CHUNK = 64
N_A = DEPTH // 2
N_B = DEPTH - N_A
N_MEM = 256
MEM_HEADS = 4
MEM_HEAD_DIM = D_MODEL // 16
MEM_W = MEM_HEADS * MEM_HEAD_DIM
GM_W = D_MODEL - MEM_W
GM_GROUPS = 6
GM_GC = GM_W // GM_GROUPS
GM_CHUNK = 128
SB_W = D_MODEL - MEM_W
SB_HEAD_DIM = 64
SB_HEADS = SB_W // SB_HEAD_DIM
SB_BLOCK = 128
D_FF = 2816
EPS = 1e-6

kernel_name = "yoco_gmlp_stickbreaking_macaron_memory"


def rms_norm(x, g):
    xf = x.astype(jnp.float32)
    r = lax.rsqrt(jnp.mean(xf * xf, axis=-1, keepdims=True) + EPS)
    return (xf * r).astype(x.dtype) * g


def half_swiglu(x, g, w_gate, w_up, w_down):
    h = rms_norm(x, g)
    return (jax.nn.silu(h @ w_gate) * (h @ w_up)) @ w_down


def memory_attention(q, mem_kv):
    b, s, _ = q.shape
    k, v = jnp.split(mem_kv, 2, axis=-1)
    q = q.reshape(b, s, MEM_HEADS, MEM_HEAD_DIM)
    k = k.reshape(b, N_MEM, MEM_HEADS, MEM_HEAD_DIM)
    v = v.reshape(b, N_MEM, MEM_HEADS, MEM_HEAD_DIM)
    scores = jnp.einsum('bshd,bmhd->bhsm', q, k).astype(jnp.float32) * (MEM_HEAD_DIM ** -0.5)
    p = jax.nn.softmax(scores, axis=-1).astype(v.dtype)
    return jnp.einsum('bhsm,bmhd->bshd', p, v).reshape(b, s, MEM_W)


def gmlp_chunk_mask():
    pos = np.arange(GM_CHUNK)
    return jnp.asarray((pos[None, :] // CHUNK) <= (pos[:, None] // CHUNK))


def gmlp_spatial_gating(u, v, v_gain, w_s, b_s):
    b, s, _ = u.shape
    v = rms_norm(v, v_gain)
    vb = v.reshape(b, s // GM_CHUNK, GM_CHUNK, GM_GROUPS, GM_GC)
    w = jnp.where(gmlp_chunk_mask()[None], w_s, 0.0).astype(v.dtype)
    mixed = jnp.einsum('gts,bnsgc->bntgc', w, vb) + b_s.T[None, None, :, :, None]
    return u * mixed.reshape(b, s, GM_W)


def stick_breaking_attention(q, k, v):
    b, s, h, d = q.shape
    nblk = s // SB_BLOCK
    qb = q.reshape(b, nblk, SB_BLOCK, h, d).transpose(1, 0, 2, 3, 4)
    starts = jnp.arange(nblk, dtype=jnp.int32) * SB_BLOCK
    key_pos = jnp.arange(s, dtype=jnp.int32)
    scale = d ** -0.5

    def block(args):
        qi, start = args
        z = jnp.einsum('bqhd,bkhd->bhqk', qi, k).astype(jnp.float32) * scale
        qpos = start + jnp.arange(SB_BLOCK, dtype=jnp.int32)
        causal = key_pos[None, :] < qpos[:, None]
        log_beta = jax.nn.log_sigmoid(z)
        log_one_minus = jnp.where(causal, jax.nn.log_sigmoid(-z), 0.0)
        suffix = lax.cumsum(log_one_minus, axis=3, reverse=True) - log_one_minus
        a = jnp.where(causal, jnp.exp(log_beta + suffix), 0.0).astype(v.dtype)
        return jnp.einsum('bhqk,bkhd->bqhd', a, v)

    out = lax.map(block, (qb, starts))
    return out.transpose(1, 0, 2, 3, 4).reshape(b, s, h * d)


def setup_inputs(seed: int = 0) -> dict:
    key = jax.random.key(seed)
    ks = iter(jax.random.split(key, 32))
    f32 = jnp.float32

    def w(shape, fan_in):
        return jax.random.normal(next(ks), shape, f32) * (fan_in ** -0.5)

    def gain(shape):
        return 1.0 + 0.02 * jax.random.normal(next(ks), shape, f32)

    return {
        "x": jax.random.normal(next(ks), (BATCH, SEQ, D_MODEL), f32),
        "mem": jax.random.normal(next(ks), (BATCH, N_MEM, D_MODEL), f32),
        "ffn1_norm": gain((DEPTH, D_MODEL)),
        "ffn1_w_gate": w((DEPTH, D_MODEL, D_FF), D_MODEL),
        "ffn1_w_up": w((DEPTH, D_MODEL, D_FF), D_MODEL),
        "ffn1_w_down": w((DEPTH, D_FF, D_MODEL), D_FF),
        "mix_norm": gain((DEPTH, D_MODEL)),
        "ffn2_norm": gain((DEPTH, D_MODEL)),
        "ffn2_w_gate": w((DEPTH, D_MODEL, D_FF), D_MODEL),
        "ffn2_w_up": w((DEPTH, D_MODEL, D_FF), D_MODEL),
        "ffn2_w_down": w((DEPTH, D_FF, D_MODEL), D_FF),
        "mem_norm": gain((D_MODEL,)),
        "w_mem_kv": w((DEPTH, D_MODEL, 2 * MEM_W), D_MODEL),
        "a_w_in": w((N_A, D_MODEL, 2 * GM_W + MEM_W), D_MODEL),
        "a_v_norm": gain((N_A, GM_W)),
        "a_w_spatial": w((N_A, GM_GROUPS, GM_CHUNK, GM_CHUNK), GM_CHUNK),
        "a_b_spatial": gain((N_A, GM_GROUPS, GM_CHUNK)),
        "a_w_out": w((N_A, GM_W + MEM_W, D_MODEL), GM_W + MEM_W),
        "kv_norm": gain((D_MODEL,)),
        "w_kv": w((D_MODEL, 2 * SB_W), D_MODEL),
        "b_w_in": w((N_B, D_MODEL, SB_W + MEM_W), D_MODEL),
        "b_w_out": w((N_B, SB_W + MEM_W, D_MODEL), SB_W + MEM_W),
        "final_norm": gain((D_MODEL,)),
    }


def reference(x, mem, ffn1_norm, ffn1_w_gate, ffn1_w_up, ffn1_w_down, mix_norm,
              ffn2_norm, ffn2_w_gate, ffn2_w_up, ffn2_w_down, mem_norm, w_mem_kv,
              a_w_in, a_v_norm, a_w_spatial, a_b_spatial, a_w_out,
              kv_norm, w_kv, b_w_in, b_w_out, final_norm):
    b, s, _ = x.shape
    mem_h = rms_norm(mem, mem_norm)
    shared_k = shared_v = None
    for l in range(DEPTH):
        if l == N_A:
            kv = rms_norm(x, kv_norm) @ w_kv
            shared_k, shared_v = jnp.split(kv, 2, axis=-1)
            shared_k = shared_k.reshape(b, s, SB_HEADS, SB_HEAD_DIM)
            shared_v = shared_v.reshape(b, s, SB_HEADS, SB_HEAD_DIM)

        x = x + 0.5 * half_swiglu(x, ffn1_norm[l], ffn1_w_gate[l], ffn1_w_up[l], ffn1_w_down[l])

        h = rms_norm(x, mix_norm[l])
        mem_kv = mem_h @ w_mem_kv[l]
        if l < N_A:
            i = l
            proj = h @ a_w_in[i]
            uv = jax.nn.gelu(proj[..., :2 * GM_W])
            u, v = uv[..., :GM_W], uv[..., GM_W:]
            q_mem = proj[..., 2 * GM_W:]
            y_tok = gmlp_spatial_gating(u, v, a_v_norm[i], a_w_spatial[i], a_b_spatial[i])
            y = jnp.concatenate([y_tok, memory_attention(q_mem, mem_kv)], axis=-1)
            x = x + y @ a_w_out[i]
        else:
            j = l - N_A
            proj = h @ b_w_in[j]
            q_sb = proj[..., :SB_W].reshape(b, s, SB_HEADS, SB_HEAD_DIM)
            q_mem = proj[..., SB_W:]
            y_tok = stick_breaking_attention(q_sb, shared_k, shared_v)
            y = jnp.concatenate([y_tok, memory_attention(q_mem, mem_kv)], axis=-1)
            x = x + y @ b_w_out[j]

        x = x + 0.5 * half_swiglu(x, ffn2_norm[l], ffn2_w_gate[l], ffn2_w_up[l], ffn2_w_down[l])
    return rms_norm(x, final_norm)
```

```python
import functools

import jax
import jax.numpy as jnp
from jax import lax
from jax.experimental import pallas as pl
from jax.experimental.pallas import tpu as pltpu

D_MODEL = 1024
DEPTH = 4
N_A = DEPTH // 2
CHUNK = 64
N_MEM = 256
MEM_HEADS = 4
HEAD_DIM = 64
MEM_W = MEM_HEADS * HEAD_DIM
GM_W = D_MODEL - MEM_W
GM_GROUPS = 6
GM_GC = GM_W // GM_GROUPS
GM_CHUNK = 128
SB_W = D_MODEL - MEM_W
D_FF = 2816
EPS = 1e-6

LANES = 128
HEAD_PAIRS_SB = SB_W // LANES
HEAD_PAIRS_MEM = MEM_W // LANES
QK_SCALE = HEAD_DIM ** -0.5

TM = 512
FF_CHUNK = 256
TQ = 256
TK = 256
VMEM_LIMIT = 56 * 1024 * 1024

BF16 = jnp.bfloat16
F32 = jnp.float32
_NT = (((1,), (1,)), ((), ()))


def _params(*sem):
    return pltpu.CompilerParams(dimension_semantics=sem, vmem_limit_bytes=VMEM_LIMIT)


def _rms(x, g):
    r = lax.rsqrt(jnp.mean(x * x, axis=-1, keepdims=True) + EPS)
    return x * r * g


def _dot(a, b):
    return jnp.dot(a, b, preferred_element_type=F32)


def _full(shape):
    return pl.BlockSpec(shape, lambda *_: (0,) * len(shape))


def _ffn_kernel(x_ref, g_ref, wg_ref, wu_ref, wd_ref, *rest, final):
    if final:
        gf_ref, o_ref, acc_ref = rest
    else:
        o_ref, acc_ref = rest
    x = x_ref[...]
    h = _rms(x, g_ref[...]).astype(BF16)
    for c in range(D_FF // FF_CHUNK):
        cols = slice(c * FF_CHUNK, (c + 1) * FF_CHUNK)
        gate = _dot(h, wg_ref[:, cols])
        up = _dot(h, wu_ref[:, cols])
        act = (gate * jax.nn.sigmoid(gate) * up).astype(BF16)
        part = _dot(act, wd_ref[cols, :])
        if c == 0:
            acc_ref[...] = part
        else:
            acc_ref[...] += part
    y = x + 0.5 * acc_ref[...]
    if final:
        y = _rms(y, gf_ref[...])
    o_ref[...] = y


def _ffn(x, g, wg, wu, wd, final_gain=None):
    t, d = x.shape
    final = final_gain is not None
    row = pl.BlockSpec((TM, d), lambda i: (i, 0))
    in_specs = [row, _full((1, d)), _full((d, D_FF)), _full((d, D_FF)), _full((D_FF, d))]
    args = [x, g.reshape(1, d), wg, wu, wd]
    if final:
        in_specs.append(_full((1, d)))
        args.append(final_gain.reshape(1, d))
    return pl.pallas_call(
        functools.partial(_ffn_kernel, final=final),
        grid=(t // TM,),
        in_specs=in_specs,
        out_specs=row,
        out_shape=jax.ShapeDtypeStruct((t, d), F32),
        scratch_shapes=[pltpu.VMEM((TM, d), F32)],
        compiler_params=_params("parallel"),
        name="ffn_final" if final else "ffn",
    )(*args)


def _mem_kv_kernel(mem_ref, g_ref, w_ref, o_ref):
    h = _rms(mem_ref[...], g_ref[...]).astype(BF16)
    o_ref[0] = _dot(h, w_ref[0].astype(BF16)).astype(BF16)


def _mem_kv(mem, g, w):
    n, d = mem.shape
    depth, _, width = w.shape
    return pl.pallas_call(
        _mem_kv_kernel,
        grid=(depth,),
        in_specs=[_full((n, d)), _full((1, d)), pl.BlockSpec((1, d, width), lambda l: (l, 0, 0))],
        out_specs=pl.BlockSpec((1, n, width), lambda l: (l, 0, 0)),
        out_shape=jax.ShapeDtypeStruct((depth, n, width), BF16),
        compiler_params=_params("parallel"),
        name="mem_kv",
    )(mem, g.reshape(1, d), w)


def _memory_attention(q_mem, kv_ref, y_ref, col0):
    rows = q_mem.shape[0]
    lane = lax.broadcasted_iota(jnp.int32, (rows, LANES), 1)
    low = lane < HEAD_DIM
    for p in range(HEAD_PAIRS_MEM):
        cols = slice(p * LANES, (p + 1) * LANES)
        q2 = q_mem[:, cols]
        k2 = kv_ref[:, cols]
        v2 = kv_ref[:, MEM_W + p * LANES:MEM_W + (p + 1) * LANES]
        outs = []
        for keep in (low, ~low):
            s = lax.dot_general(jnp.where(keep, q2, jnp.zeros_like(q2)), k2, _NT,
                                preferred_element_type=F32)
            e = jnp.exp(s - jnp.max(s, axis=-1, keepdims=True))
            inv = 1.0 / jnp.sum(e, axis=-1, keepdims=True)
            outs.append(_dot(e.astype(BF16), v2) * inv)
        y_ref[:, col0 + p * LANES:col0 + (p + 1) * LANES] = jnp.where(low, outs[0], outs[1]).astype(BF16)


def _mixer_a_kernel(x_ref, g_ref, win_ref, vg_ref, ws_ref, bs_ref, kv_ref, wout_ref, o_ref, y_ref):
    x = x_ref[...]
    h = _rms(x, g_ref[...]).astype(BF16)
    proj = _dot(h, win_ref[...])
    u = jax.nn.gelu(proj[:, :GM_W])
    v = _rms(jax.nn.gelu(proj[:, GM_W:2 * GM_W]), vg_ref[...]).astype(BF16)
    pos_t = lax.broadcasted_iota(jnp.int32, (GM_CHUNK, GM_CHUNK), 0) // CHUNK
    pos_s = lax.broadcasted_iota(jnp.int32, (GM_CHUNK, GM_CHUNK), 1) // CHUNK
    chunk_causal = pos_s <= pos_t
    for gi in range(GM_GROUPS):
        cols = slice(gi * GM_GC, (gi + 1) * GM_GC)
        w = jnp.where(chunk_causal, ws_ref[gi], 0.0).astype(BF16)
        bias = bs_ref[gi]
        for n in range(TM // GM_CHUNK):
            rows = slice(n * GM_CHUNK, (n + 1) * GM_CHUNK)
            mixed = _dot(w, v[rows, cols]) + bias
            y_ref[rows, cols] = (u[rows, cols] * mixed).astype(BF16)
    q_mem = (proj[:, 2 * GM_W:] * QK_SCALE).astype(BF16)
    _memory_attention(q_mem, kv_ref.at[0], y_ref, GM_W)
    o_ref[...] = x + _dot(y_ref[...], wout_ref[...])


def _mixer_a(x, g, w_in, v_gain, w_sp, b_sp, mem_kv, w_out, seq):
    t, d = x.shape
    steps_per_batch = seq // TM
    row = pl.BlockSpec((TM, d), lambda i: (i, 0))
    return pl.pallas_call(
        _mixer_a_kernel,
        grid=(t // TM,),
        in_specs=[
            row, _full((1, d)), _full(w_in.shape), _full((1, GM_W)),
            _full(w_sp.shape), _full((GM_GROUPS, GM_CHUNK, 1)),
            pl.BlockSpec((1, N_MEM, 2 * MEM_W), lambda i: (i // steps_per_batch, 0, 0)),
            _full(w_out.shape),
        ],
        out_specs=row,
        out_shape=jax.ShapeDtypeStruct((t, d), F32),
        scratch_shapes=[pltpu.VMEM((TM, d), BF16)],
        compiler_params=_params("parallel"),
        name="mixer_a",
    )(x, g.reshape(1, d), w_in, v_gain.reshape(1, GM_W), w_sp,
      b_sp.reshape(GM_GROUPS, GM_CHUNK, 1), mem_kv, w_out)


def _kv_proj_kernel(x_ref, g_ref, w_ref, k_ref, v_ref):
    h = _rms(x_ref[...], g_ref[...]).astype(BF16)
    kv = _dot(h, w_ref[...])
    k_ref[...] = kv[:, :SB_W].astype(BF16)
    v_ref[...] = kv[:, SB_W:].astype(BF16)


def _kv_proj(x, g, w):
    t, d = x.shape
    out = pl.BlockSpec((TM, SB_W), lambda i: (i, 0))
    return pl.pallas_call(
        _kv_proj_kernel,
        grid=(t // TM,),
        in_specs=[pl.BlockSpec((TM, d), lambda i: (i, 0)), _full((1, d)), _full(w.shape)],
        out_specs=[out, out],
        out_shape=[jax.ShapeDtypeStruct((t, SB_W), BF16)] * 2,
        compiler_params=_params("parallel"),
        name="kv_proj",
    )(x, g.reshape(1, d), w)


def _b_in_kernel(x_ref, g_ref, w_ref, qsb_ref, qmem_ref):
    h = _rms(x_ref[...], g_ref[...]).astype(BF16)
    q = _dot(h, w_ref[...]) * QK_SCALE
    qsb_ref[...] = q[:, :SB_W].astype(BF16)
    qmem_ref[...] = q[:, SB_W:].astype(BF16)


def _b_in(x, g, w):
    t, d = x.shape
    return pl.pallas_call(
        _b_in_kernel,
        grid=(t // TM,),
        in_specs=[pl.BlockSpec((TM, d), lambda i: (i, 0)), _full((1, d)), _full(w.shape)],
        out_specs=[pl.BlockSpec((TM, SB_W), lambda i: (i, 0)), pl.BlockSpec((TM, MEM_W), lambda i: (i, 0))],
        out_shape=[jax.ShapeDtypeStruct((t, SB_W), BF16), jax.ShapeDtypeStruct((t, MEM_W), BF16)],
        compiler_params=_params("parallel"),
        name="b_in",
    )(x, g.reshape(1, d), w)


def _b_out_kernel(x_ref, ytok_ref, qmem_ref, kv_ref, wout_ref, o_ref, y_ref):
    y_ref[:, :SB_W] = ytok_ref[...]
    _memory_attention(qmem_ref[...], kv_ref.at[0], y_ref, SB_W)
    o_ref[...] = x_ref[...] + _dot(y_ref[...], wout_ref[...])


def _b_out(x, y_tok, q_mem, mem_kv, w_out, seq):
    t, d = x.shape
    steps_per_batch = seq // TM
    row = pl.BlockSpec((TM, d), lambda i: (i, 0))
    return pl.pallas_call(
        _b_out_kernel,
        grid=(t // TM,),
        in_specs=[
            row,
            pl.BlockSpec((TM, SB_W), lambda i: (i, 0)),
            pl.BlockSpec((TM, MEM_W), lambda i: (i, 0)),
            pl.BlockSpec((1, N_MEM, 2 * MEM_W), lambda i: (i // steps_per_batch, 0, 0)),
            _full(w_out.shape),
        ],
        out_specs=row,
        out_shape=jax.ShapeDtypeStruct((t, d), F32),
        scratch_shapes=[pltpu.VMEM((TM, d), BF16)],
        compiler_params=_params("parallel"),
        name="b_out",
    )(x, y_tok, q_mem, mem_kv, w_out)


def _softplus(z):
    return jnp.maximum(z, 0.0) + jnp.log(1.0 + jnp.exp(-jnp.abs(z)))


def _sb_attn_kernel(q_ref, k_ref, v_ref, o_ref):
    i = pl.program_id(2)
    q2 = q_ref[0]
    lane = lax.broadcasted_iota(jnp.int32, (TQ, LANES), 1)
    low = lane < HEAD_DIM
    row = lax.broadcasted_iota(jnp.int32, (TQ, TK), 0)
    col = lax.broadcasted_iota(jnp.int32, (TQ, TK), 1)
    strictly_causal = col < row
    kj = lax.broadcasted_iota(jnp.int32, (TK, TK), 0)
    ks = lax.broadcasted_iota(jnp.int32, (TK, TK), 1)
    suffix_ones = (kj >= ks).astype(BF16)

    def tile(qh, j, c, acc, mask):
        start = pl.multiple_of(j * TK, TK)
        kt = k_ref[0, pl.ds(start, TK), :]
        vt = v_ref[0, pl.ds(start, TK), :]
        z = lax.dot_general(qh, kt, _NT, preferred_element_type=F32)
        sp = _softplus(z)
        if mask is not None:
            sp = jnp.where(mask, sp, 0.0)
        hi = sp.astype(BF16)
        lo = (sp - hi.astype(F32)).astype(BF16)
        suf = _dot(hi, suffix_ones) + _dot(lo, suffix_ones)
        a = jnp.exp(z - suf - c)
        if mask is not None:
            a = jnp.where(mask, a, 0.0)
        acc = acc + _dot(a.astype(BF16), vt)
        return c + suf[:, :1], acc

    outs = []
    for keep in (low, ~low):
        qh = jnp.where(keep, q2, jnp.zeros_like(q2))
        c, acc = tile(qh, i, jnp.zeros((TQ, 1), F32), jnp.zeros((TQ, LANES), F32), strictly_causal)

        def body(n, carry, qh=qh):
            return tile(qh, i - 1 - n, carry[0], carry[1], None)

        c, acc = lax.fori_loop(0, i, body, (c, acc))
        outs.append(acc)
    o_ref[0] = jnp.where(low, outs[0], outs[1]).astype(BF16)


def _sb_attn(q, k, v):
    b, s, _ = q.shape
    qspec = pl.BlockSpec((1, TQ, LANES), lambda bi, p, i: (bi, i, p))
    kvspec = pl.BlockSpec((1, s, LANES), lambda bi, p, i: (bi, 0, p))
    return pl.pallas_call(
        _sb_attn_kernel,
        grid=(b, HEAD_PAIRS_SB, s // TQ),
        in_specs=[qspec, kvspec, kvspec],
        out_specs=qspec,
        out_shape=jax.ShapeDtypeStruct((b, s, SB_W), BF16),
        compiler_params=_params("parallel", "parallel", "arbitrary"),
        name="sb_attn",
    )(q, k, v)


def kernel(x, mem, ffn1_norm, ffn1_w_gate, ffn1_w_up, ffn1_w_down, mix_norm, ffn2_norm, ffn2_w_gate, ffn2_w_up, ffn2_w_down, mem_norm, w_mem_kv, a_w_in, a_v_norm, a_w_spatial, a_b_spatial, a_w_out, kv_norm, w_kv, b_w_in, b_w_out, final_norm):
    b, s, d = x.shape
    assert d == D_MODEL and s % TM == 0 and s % TQ == 0 and TQ == TK
    t = b * s
    xt = x.reshape(t, d)
    mem_kv = _mem_kv(mem.reshape(b * N_MEM, d), mem_norm, w_mem_kv)
    mem_kv = mem_kv.reshape(DEPTH * b, N_MEM, 2 * MEM_W)
    k = v = None
    for l in range(DEPTH):
        if l == N_A:
            k, v = _kv_proj(xt, kv_norm, w_kv.astype(BF16))
            k = k.reshape(b, s, SB_W)
            v = v.reshape(b, s, SB_W)
        xt = _ffn(xt, ffn1_norm[l], ffn1_w_gate[l].astype(BF16), ffn1_w_up[l].astype(BF16),
                  ffn1_w_down[l].astype(BF16))
        kv_l = mem_kv[l * b:(l + 1) * b]
        if l < N_A:
            xt = _mixer_a(xt, mix_norm[l], a_w_in[l].astype(BF16), a_v_norm[l], a_w_spatial[l],
                          a_b_spatial[l], kv_l, a_w_out[l].astype(BF16), s)
        else:
            j = l - N_A
            q_sb, q_mem = _b_in(xt, mix_norm[l], b_w_in[j].astype(BF16))
            y_tok = _sb_attn(q_sb.reshape(b, s, SB_W), k, v).reshape(t, SB_W)
            xt = _b_out(xt, y_tok, q_mem, kv_l, b_w_out[j].astype(BF16), s)
        xt = _ffn(xt, ffn2_norm[l], ffn2_w_gate[l].astype(BF16), ffn2_w_up[l].astype(BF16),
                  ffn2_w_down[l].astype(BF16), final_gain=final_norm if l == DEPTH - 1 else None)
    return xt.reshape(b, s, d)
```

```python
import functools

import jax
import jax.numpy as jnp
from jax import lax
from jax.experimental import pallas as pl
from jax.experimental.pallas import tpu as pltpu

D_MODEL = 1024
DEPTH = 4
N_A = DEPTH // 2
CHUNK = 64
N_MEM = 256
MEM_HEADS = 4
HEAD_DIM = 64
MEM_W = MEM_HEADS * HEAD_DIM
GM_W = D_MODEL - MEM_W
GM_GROUPS = 6
GM_GC = GM_W // GM_GROUPS
GM_CHUNK = 128
SB_W = D_MODEL - MEM_W
D_FF = 2816
EPS = 1e-6

LANES = 128
HEAD_PAIRS_SB = SB_W // LANES
HEAD_PAIRS_MEM = MEM_W // LANES
QK_SCALE = HEAD_DIM ** -0.5

TM = 512
FF_CHUNK = 256
TQ = 256
TK = 256
SB_PAIRS_PER_STEP = 2
SB_DONE = 110.0
VMEM_LIMIT = 56 * 1024 * 1024

BF16 = jnp.bfloat16
F32 = jnp.float32
_NT = (((1,), (1,)), ((), ()))


def _params(*sem):
    return pltpu.CompilerParams(dimension_semantics=sem, vmem_limit_bytes=VMEM_LIMIT)


def _rms(x, g):
    r = lax.rsqrt(jnp.mean(x * x, axis=-1, keepdims=True) + EPS)
    return x * r * g


def _dot(a, b):
    return jnp.dot(a, b, preferred_element_type=F32)


def _full(shape):
    return pl.BlockSpec(shape, lambda *_: (0,) * len(shape))


def _ffn_kernel(x_ref, g_ref, wg_ref, wu_ref, wd_ref, *rest, final):
    if final:
        gf_ref, o_ref, acc_ref = rest
    else:
        o_ref, acc_ref = rest
    x = x_ref[...]
    h = _rms(x, g_ref[...]).astype(BF16)
    for c in range(D_FF // FF_CHUNK):
        cols = slice(c * FF_CHUNK, (c + 1) * FF_CHUNK)
        gate = _dot(h, wg_ref[:, cols])
        up = _dot(h, wu_ref[:, cols])
        act = (gate * jax.nn.sigmoid(gate) * up).astype(BF16)
        part = _dot(act, wd_ref[cols, :])
        if c == 0:
            acc_ref[...] = part
        else:
            acc_ref[...] += part
    y = x + 0.5 * acc_ref[...]
    if final:
        y = _rms(y, gf_ref[...])
    o_ref[...] = y


def _ffn(x, g, wg, wu, wd, final_gain=None):
    t, d = x.shape
    final = final_gain is not None
    row = pl.BlockSpec((TM, d), lambda i: (i, 0))
    in_specs = [row, _full((1, d)), _full((d, D_FF)), _full((d, D_FF)), _full((D_FF, d))]
    args = [x, g.reshape(1, d), wg, wu, wd]
    if final:
        in_specs.append(_full((1, d)))
        args.append(final_gain.reshape(1, d))
    return pl.pallas_call(
        functools.partial(_ffn_kernel, final=final),
        grid=(t // TM,),
        in_specs=in_specs,
        out_specs=row,
        out_shape=jax.ShapeDtypeStruct((t, d), F32),
        scratch_shapes=[pltpu.VMEM((TM, d), F32)],
        compiler_params=_params("parallel"),
        name="ffn_final" if final else "ffn",
    )(*args)


def _mem_kv_kernel(mem_ref, g_ref, w_ref, o_ref):
    h = _rms(mem_ref[...], g_ref[...]).astype(BF16)
    o_ref[0] = _dot(h, w_ref[0].astype(BF16)).astype(BF16)


def _mem_kv(mem, g, w):
    n, d = mem.shape
    depth, _, width = w.shape
    return pl.pallas_call(
        _mem_kv_kernel,
        grid=(depth,),
        in_specs=[_full((n, d)), _full((1, d)), pl.BlockSpec((1, d, width), lambda l: (l, 0, 0))],
        out_specs=pl.BlockSpec((1, n, width), lambda l: (l, 0, 0)),
        out_shape=jax.ShapeDtypeStruct((depth, n, width), BF16),
        compiler_params=_params("parallel"),
        name="mem_kv",
    )(mem, g.reshape(1, d), w)


def _memory_attention(q_mem, kv_ref, y_ref, col0):
    rows = q_mem.shape[0]
    lane = lax.broadcasted_iota(jnp.int32, (rows, LANES), 1)
    low = lane < HEAD_DIM
    for p in range(HEAD_PAIRS_MEM):
        cols = slice(p * LANES, (p + 1) * LANES)
        q2 = q_mem[:, cols]
        k2 = kv_ref[:, cols]
        v2 = kv_ref[:, MEM_W + p * LANES:MEM_W + (p + 1) * LANES]
        outs = []
        for keep in (low, ~low):
            s = lax.dot_general(jnp.where(keep, q2, jnp.zeros_like(q2)), k2, _NT,
                                preferred_element_type=F32)
            e = jnp.exp(s - jnp.max(s, axis=-1, keepdims=True))
            inv = 1.0 / jnp.sum(e, axis=-1, keepdims=True)
            outs.append(_dot(e.astype(BF16), v2) * inv)
        y_ref[:, col0 + p * LANES:col0 + (p + 1) * LANES] = jnp.where(low, outs[0], outs[1]).astype(BF16)


def _mixer_a_kernel(x_ref, g_ref, win_ref, vg_ref, ws_ref, bs_ref, kv_ref, wout_ref, o_ref, y_ref):
    x = x_ref[...]
    h = _rms(x, g_ref[...]).astype(BF16)
    proj = _dot(h, win_ref[...])
    u = jax.nn.gelu(proj[:, :GM_W])
    v = _rms(jax.nn.gelu(proj[:, GM_W:2 * GM_W]), vg_ref[...]).astype(BF16)
    pos_t = lax.broadcasted_iota(jnp.int32, (GM_CHUNK, GM_CHUNK), 0) // CHUNK
    pos_s = lax.broadcasted_iota(jnp.int32, (GM_CHUNK, GM_CHUNK), 1) // CHUNK
    chunk_causal = pos_s <= pos_t
    for gi in range(GM_GROUPS):
        cols = slice(gi * GM_GC, (gi + 1) * GM_GC)
        w = jnp.where(chunk_causal, ws_ref[gi], 0.0).astype(BF16)
        bias = bs_ref[gi]
        for n in range(TM // GM_CHUNK):
            rows = slice(n * GM_CHUNK, (n + 1) * GM_CHUNK)
            mixed = _dot(w, v[rows, cols]) + bias
            y_ref[rows, cols] = (u[rows, cols] * mixed).astype(BF16)
    q_mem = (proj[:, 2 * GM_W:] * QK_SCALE).astype(BF16)
    _memory_attention(q_mem, kv_ref.at[0], y_ref, GM_W)
    o_ref[...] = x + _dot(y_ref[...], wout_ref[...])


def _mixer_a(x, g, w_in, v_gain, w_sp, b_sp, mem_kv, w_out, seq):
    t, d = x.shape
    steps_per_batch = seq // TM
    row = pl.BlockSpec((TM, d), lambda i: (i, 0))
    return pl.pallas_call(
        _mixer_a_kernel,
        grid=(t // TM,),
        in_specs=[
            row, _full((1, d)), _full(w_in.shape), _full((1, GM_W)),
            _full(w_sp.shape), _full((GM_GROUPS, GM_CHUNK, 1)),
            pl.BlockSpec((1, N_MEM, 2 * MEM_W), lambda i: (i // steps_per_batch, 0, 0)),
            _full(w_out.shape),
        ],
        out_specs=row,
        out_shape=jax.ShapeDtypeStruct((t, d), F32),
        scratch_shapes=[pltpu.VMEM((TM, d), BF16)],
        compiler_params=_params("parallel"),
        name="mixer_a",
    )(x, g.reshape(1, d), w_in, v_gain.reshape(1, GM_W), w_sp,
      b_sp.reshape(GM_GROUPS, GM_CHUNK, 1), mem_kv, w_out)


def _kv_proj_kernel(x_ref, g_ref, w_ref, k_ref, v_ref):
    h = _rms(x_ref[...], g_ref[...]).astype(BF16)
    kv = _dot(h, w_ref[...])
    k_ref[...] = kv[:, :SB_W].astype(BF16)
    v_ref[...] = kv[:, SB_W:].astype(BF16)


def _kv_proj(x, g, w):
    t, d = x.shape
    out = pl.BlockSpec((TM, SB_W), lambda i: (i, 0))
    return pl.pallas_call(
        _kv_proj_kernel,
        grid=(t // TM,),
        in_specs=[pl.BlockSpec((TM, d), lambda i: (i, 0)), _full((1, d)), _full(w.shape)],
        out_specs=[out, out],
        out_shape=[jax.ShapeDtypeStruct((t, SB_W), BF16)] * 2,
        compiler_params=_params("parallel"),
        name="kv_proj",
    )(x, g.reshape(1, d), w)


def _b_in_kernel(x_ref, g_ref, w_ref, qsb_ref, qmem_ref):
    h = _rms(x_ref[...], g_ref[...]).astype(BF16)
    q = _dot(h, w_ref[...]) * QK_SCALE
    qsb_ref[...] = q[:, :SB_W].astype(BF16)
    qmem_ref[...] = q[:, SB_W:].astype(BF16)


def _b_in(x, g, w):
    t, d = x.shape
    return pl.pallas_call(
        _b_in_kernel,
        grid=(t // TM,),
        in_specs=[pl.BlockSpec((TM, d), lambda i: (i, 0)), _full((1, d)), _full(w.shape)],
        out_specs=[pl.BlockSpec((TM, SB_W), lambda i: (i, 0)), pl.BlockSpec((TM, MEM_W), lambda i: (i, 0))],
        out_shape=[jax.ShapeDtypeStruct((t, SB_W), BF16), jax.ShapeDtypeStruct((t, MEM_W), BF16)],
        compiler_params=_params("parallel"),
        name="b_in",
    )(x, g.reshape(1, d), w)


def _b_out_kernel(x_ref, ytok_ref, qmem_ref, kv_ref, wout_ref, o_ref, y_ref):
    y_ref[:, :SB_W] = ytok_ref[...]
    _memory_attention(qmem_ref[...], kv_ref.at[0], y_ref, SB_W)
    o_ref[...] = x_ref[...] + _dot(y_ref[...], wout_ref[...])


def _b_out(x, y_tok, q_mem, mem_kv, w_out, seq):
    t, d = x.shape
    steps_per_batch = seq // TM
    row = pl.BlockSpec((TM, d), lambda i: (i, 0))
    return pl.pallas_call(
        _b_out_kernel,
        grid=(t // TM,),
        in_specs=[
            row,
            pl.BlockSpec((TM, SB_W), lambda i: (i, 0)),
            pl.BlockSpec((TM, MEM_W), lambda i: (i, 0)),
            pl.BlockSpec((1, N_MEM, 2 * MEM_W), lambda i: (i // steps_per_batch, 0, 0)),
            _full(w_out.shape),
        ],
        out_specs=row,
        out_shape=jax.ShapeDtypeStruct((t, d), F32),
        scratch_shapes=[pltpu.VMEM((TM, d), BF16)],
        compiler_params=_params("parallel"),
        name="b_out",
    )(x, y_tok, q_mem, mem_kv, w_out)


def _softplus(z):
    return jnp.maximum(z, 0.0) + jnp.log(1.0 + jnp.exp(-jnp.abs(z)))


def _sb_attn_kernel(q_ref, k_ref, v_ref, o_ref):
    i = pl.program_id(2)
    lane = lax.broadcasted_iota(jnp.int32, (TQ, LANES), 1)
    low = lane < HEAD_DIM
    row = lax.broadcasted_iota(jnp.int32, (TQ, TK), 0)
    col = lax.broadcasted_iota(jnp.int32, (TQ, TK), 1)
    strictly_causal = col < row
    kj = lax.broadcasted_iota(jnp.int32, (TK, TK), 0)
    ks = lax.broadcasted_iota(jnp.int32, (TK, TK), 1)
    suffix_ones = (kj >= ks).astype(BF16)
    suffix_ones2 = jnp.concatenate([suffix_ones, suffix_ones], axis=0)

    qhs = []
    for p in range(SB_PAIRS_PER_STEP):
        q2 = q_ref[0, :, p * LANES:(p + 1) * LANES]
        qhs += [jnp.where(keep, q2, jnp.zeros_like(q2)) for keep in (low, ~low)]

    n_chains = 2 * SB_PAIRS_PER_STEP

    def sweep(j, cs, accs, mask):
        start = pl.multiple_of(j * TK, TK)
        kts = [k_ref[0, pl.ds(start, TK), p * LANES:(p + 1) * LANES] for p in range(SB_PAIRS_PER_STEP)]
        vts = [v_ref[0, pl.ds(start, TK), p * LANES:(p + 1) * LANES] for p in range(SB_PAIRS_PER_STEP)]
        zs = [lax.dot_general(qhs[ch], kts[ch // 2], _NT, preferred_element_type=F32) for ch in range(n_chains)]
        sps = [_softplus(z) for z in zs]
        if mask is not None:
            sps = [jnp.where(mask, sp, 0.0) for sp in sps]
        his = [sp.astype(BF16) for sp in sps]
        los = [(sp - hi.astype(F32)).astype(BF16) for sp, hi in zip(sps, his)]
        sufs = [_dot(jnp.concatenate([hi, lo], axis=1), suffix_ones2) for hi, lo in zip(his, los)]
        ws = [jnp.exp(z - suf - c) for z, suf, c in zip(zs, sufs, cs)]
        if mask is not None:
            ws = [jnp.where(mask, w, 0.0) for w in ws]
        accs = tuple(acc + _dot(w.astype(BF16), vts[ch // 2]) for ch, (acc, w) in enumerate(zip(accs, ws)))
        cs = tuple(c + suf[:, :1] for c, suf in zip(cs, sufs))
        return cs, accs

    cs, accs = sweep(i, (jnp.zeros((TQ, 1), F32),) * n_chains, (jnp.zeros((TQ, LANES), F32),) * n_chains,
                     strictly_causal)

    def keep_going(carry):
        n, c_min, _, _ = carry
        return jnp.logical_and(n < i, c_min < SB_DONE)

    def step(carry):
        n, _, cs, accs = carry
        cs, accs = sweep(i - 1 - n, cs, accs, None)
        c_min = functools.reduce(jnp.minimum, [jnp.min(c) for c in cs])
        return n + 1, c_min, cs, accs

    _, _, _, accs = lax.while_loop(keep_going, step, (jnp.int32(0), jnp.float32(0.0), cs, accs))
    for p in range(SB_PAIRS_PER_STEP):
        o_ref[0, :, p * LANES:(p + 1) * LANES] = jnp.where(low, accs[2 * p], accs[2 * p + 1]).astype(BF16)


def _sb_attn(q, k, v):
    b, s, _ = q.shape
    width = SB_PAIRS_PER_STEP * LANES
    qspec = pl.BlockSpec((1, TQ, width), lambda bi, p, i: (bi, i, p))
    kvspec = pl.BlockSpec((1, s, width), lambda bi, p, i: (bi, 0, p))
    return pl.pallas_call(
        _sb_attn_kernel,
        grid=(b, HEAD_PAIRS_SB // SB_PAIRS_PER_STEP, s // TQ),
        in_specs=[qspec, kvspec, kvspec],
        out_specs=qspec,
        out_shape=jax.ShapeDtypeStruct((b, s, SB_W), BF16),
        compiler_params=_params("parallel", "parallel", "arbitrary"),
        name="sb_attn",
    )(q, k, v)


def kernel(x, mem, ffn1_norm, ffn1_w_gate, ffn1_w_up, ffn1_w_down, mix_norm, ffn2_norm, ffn2_w_gate, ffn2_w_up, ffn2_w_down, mem_norm, w_mem_kv, a_w_in, a_v_norm, a_w_spatial, a_b_spatial, a_w_out, kv_norm, w_kv, b_w_in, b_w_out, final_norm):
    b, s, d = x.shape
    assert d == D_MODEL and s % TM == 0 and s % TQ == 0 and TQ == TK
    t = b * s
    xt = x.reshape(t, d)
    mem_kv = _mem_kv(mem.reshape(b * N_MEM, d), mem_norm, w_mem_kv)
    mem_kv = mem_kv.reshape(DEPTH * b, N_MEM, 2 * MEM_W)
    k = v = None
    for l in range(DEPTH):
        if l == N_A:
            k, v = _kv_proj(xt, kv_norm, w_kv.astype(BF16))
            k = k.reshape(b, s, SB_W)
            v = v.reshape(b, s, SB_W)
        xt = _ffn(xt, ffn1_norm[l], ffn1_w_gate[l].astype(BF16), ffn1_w_up[l].astype(BF16),
                  ffn1_w_down[l].astype(BF16))
        kv_l = mem_kv[l * b:(l + 1) * b]
        if l < N_A:
            xt = _mixer_a(xt, mix_norm[l], a_w_in[l].astype(BF16), a_v_norm[l], a_w_spatial[l],
                          a_b_spatial[l], kv_l, a_w_out[l].astype(BF16), s)
        else:
            j = l - N_A
            q_sb, q_mem = _b_in(xt, mix_norm[l], b_w_in[j].astype(BF16))
            y_tok = _sb_attn(q_sb.reshape(b, s, SB_W), k, v).reshape(t, SB_W)
            xt = _b_out(xt, y_tok, q_mem, kv_l, b_w_out[j].astype(BF16), s)
        xt = _ffn(xt, ffn2_norm[l], ffn2_w_gate[l].astype(BF16), ffn2_w_up[l].astype(BF16),
                  ffn2_w_down[l].astype(BF16), final_gain=final_norm if l == DEPTH - 1 else None)
    return xt.reshape(b, s, d)
```

```python
import functools

import jax
import jax.numpy as jnp
from jax import lax
from jax.experimental import pallas as pl
from jax.experimental.pallas import tpu as pltpu

D_MODEL = 1024
DEPTH = 4
N_A = DEPTH // 2
CHUNK = 64
N_MEM = 256
MEM_HEADS = 4
HEAD_DIM = 64
MEM_W = MEM_HEADS * HEAD_DIM
GM_W = D_MODEL - MEM_W
GM_GROUPS = 6
GM_GC = GM_W // GM_GROUPS
GM_CHUNK = 128
SB_W = D_MODEL - MEM_W
D_FF = 2816
EPS = 1e-6

LANES = 128
HEAD_PAIRS_SB = SB_W // LANES
HEAD_PAIRS_MEM = MEM_W // LANES
QK_SCALE = HEAD_DIM ** -0.5
LOG2E = 1.4426950408889634

TM = 512
FF_CHUNK = 256
TQ = 256
TK = 256
SB_PAIRS_PER_STEP = 2
SB_DONE = 158.0
SB_NO_TILE = 1e30
VMEM_LIMIT = 56 * 1024 * 1024

BF16 = jnp.bfloat16
F32 = jnp.float32
_NT = (((1,), (1,)), ((), ()))


def _params(*sem):
    return pltpu.CompilerParams(dimension_semantics=sem, vmem_limit_bytes=VMEM_LIMIT)


def _rms(x, g):
    r = lax.rsqrt(jnp.mean(x * x, axis=-1, keepdims=True) + EPS)
    return x * r * g


def _dot(a, b):
    return jnp.dot(a, b, preferred_element_type=F32)


def _full(shape):
    return pl.BlockSpec(shape, lambda *_: (0,) * len(shape))


def _layer_block(shape, layer):
    return pl.BlockSpec((1,) + tuple(shape), lambda *_: (layer,) + (0,) * len(shape),
                        pipeline_mode=pl.Buffered(1))


def _mem_kv_block(layer, batch, seq):
    steps_per_batch = seq // TM
    return pl.BlockSpec((1, N_MEM, 2 * MEM_W), lambda i: (layer * batch + i // steps_per_batch, 0, 0))


def _ffn_kernel(x_ref, g_ref, wg_ref, wu_ref, wd_ref, *rest, final):
    if final:
        gf_ref, o_ref, acc_ref = rest
    else:
        o_ref, acc_ref = rest
    x = x_ref[...]
    h = _rms(x, g_ref[0]).astype(BF16)
    for c in range(D_FF // FF_CHUNK):
        cols = slice(c * FF_CHUNK, (c + 1) * FF_CHUNK)
        gate = _dot(h, wg_ref[0, :, cols].astype(BF16))
        up = _dot(h, wu_ref[0, :, cols].astype(BF16))
        act = (gate * jax.nn.sigmoid(gate) * up).astype(BF16)
        part = _dot(act, wd_ref[0, cols, :].astype(BF16))
        if c == 0:
            acc_ref[...] = part
        else:
            acc_ref[...] += part
    y = x + 0.5 * acc_ref[...]
    if final:
        y = _rms(y, gf_ref[...])
    o_ref[...] = y


def _ffn(x, g, wg, wu, wd, layer, final_gain=None):
    t, d = x.shape
    final = final_gain is not None
    row = pl.BlockSpec((TM, d), lambda i: (i, 0))
    in_specs = [row, _layer_block((1, d), layer), _layer_block((d, D_FF), layer),
                _layer_block((d, D_FF), layer), _layer_block((D_FF, d), layer)]
    args = [x, g.reshape(-1, 1, d), wg, wu, wd]
    if final:
        in_specs.append(_full((1, d)))
        args.append(final_gain.reshape(1, d))
    return pl.pallas_call(
        functools.partial(_ffn_kernel, final=final),
        grid=(t // TM,),
        in_specs=in_specs,
        out_specs=row,
        out_shape=jax.ShapeDtypeStruct((t, d), F32),
        scratch_shapes=[pltpu.VMEM((TM, d), F32)],
        compiler_params=_params("parallel"),
        name="ffn_final" if final else "ffn",
    )(*args)


def _mem_kv_kernel(mem_ref, g_ref, w_ref, o_ref):
    h = _rms(mem_ref[...], g_ref[...]).astype(BF16)
    o_ref[0] = _dot(h, w_ref[0].astype(BF16)).astype(BF16)


def _mem_kv(mem, g, w):
    n, d = mem.shape
    depth, _, width = w.shape
    return pl.pallas_call(
        _mem_kv_kernel,
        grid=(depth,),
        in_specs=[_full((n, d)), _full((1, d)), pl.BlockSpec((1, d, width), lambda l: (l, 0, 0))],
        out_specs=pl.BlockSpec((1, n, width), lambda l: (l, 0, 0)),
        out_shape=jax.ShapeDtypeStruct((depth, n, width), BF16),
        compiler_params=_params("parallel"),
        name="mem_kv",
    )(mem, g.reshape(1, d), w)


def _memory_attention(q_mem, kv_ref, y_ref, col0):
    rows = q_mem.shape[0]
    lane = lax.broadcasted_iota(jnp.int32, (rows, LANES), 1)
    low = lane < HEAD_DIM
    for p in range(HEAD_PAIRS_MEM):
        cols = slice(p * LANES, (p + 1) * LANES)
        q2 = q_mem[:, cols]
        k2 = kv_ref[:, cols]
        v2 = kv_ref[:, MEM_W + p * LANES:MEM_W + (p + 1) * LANES]
        outs = []
        for keep in (low, ~low):
            s = lax.dot_general(jnp.where(keep, q2, jnp.zeros_like(q2)), k2, _NT,
                                preferred_element_type=F32)
            e = jnp.exp(s - jnp.max(s, axis=-1, keepdims=True))
            inv = 1.0 / jnp.sum(e, axis=-1, keepdims=True)
            outs.append(_dot(e.astype(BF16), v2) * inv)
        y_ref[:, col0 + p * LANES:col0 + (p + 1) * LANES] = jnp.where(low, outs[0], outs[1]).astype(BF16)


def _mixer_a_kernel(x_ref, g_ref, win_ref, vg_ref, ws_ref, bs_ref, kv_ref, wout_ref, o_ref, y_ref):
    x = x_ref[...]
    h = _rms(x, g_ref[0]).astype(BF16)
    proj = _dot(h, win_ref[0].astype(BF16))
    u = jax.nn.gelu(proj[:, :GM_W])
    v = _rms(jax.nn.gelu(proj[:, GM_W:2 * GM_W]), vg_ref[0]).astype(BF16)
    pos_t = lax.broadcasted_iota(jnp.int32, (GM_CHUNK, GM_CHUNK), 0) // CHUNK
    pos_s = lax.broadcasted_iota(jnp.int32, (GM_CHUNK, GM_CHUNK), 1) // CHUNK
    chunk_causal = pos_s <= pos_t
    for gi in range(GM_GROUPS):
        cols = slice(gi * GM_GC, (gi + 1) * GM_GC)
        w = jnp.where(chunk_causal, ws_ref[0, gi], 0.0).astype(BF16)
        bias = bs_ref[0, gi]
        for n in range(TM // GM_CHUNK):
            rows = slice(n * GM_CHUNK, (n + 1) * GM_CHUNK)
            mixed = _dot(w, v[rows, cols]) + bias
            y_ref[rows, cols] = (u[rows, cols] * mixed).astype(BF16)
    q_mem = (proj[:, 2 * GM_W:] * QK_SCALE).astype(BF16)
    _memory_attention(q_mem, kv_ref.at[0], y_ref, GM_W)
    o_ref[...] = x + _dot(y_ref[...], wout_ref[0].astype(BF16))


def _mixer_a(x, g, w_in, v_gain, w_sp, b_sp, mem_kv, w_out, layer, batch):
    t, d = x.shape
    row = pl.BlockSpec((TM, d), lambda i: (i, 0))
    return pl.pallas_call(
        _mixer_a_kernel,
        grid=(t // TM,),
        in_specs=[
            row, _layer_block((1, d), layer), _layer_block(w_in.shape[1:], layer),
            _layer_block((1, GM_W), layer), _layer_block(w_sp.shape[1:], layer),
            _layer_block((GM_GROUPS, GM_CHUNK, 1), layer),
            _mem_kv_block(layer, batch, t // batch),
            _layer_block(w_out.shape[1:], layer),
        ],
        out_specs=row,
        out_shape=jax.ShapeDtypeStruct((t, d), F32),
        scratch_shapes=[pltpu.VMEM((TM, d), BF16)],
        compiler_params=_params("parallel"),
        name="mixer_a",
    )(x, g.reshape(-1, 1, d), w_in, v_gain.reshape(-1, 1, GM_W), w_sp,
      b_sp.reshape(-1, GM_GROUPS, GM_CHUNK, 1), mem_kv, w_out)


def _kv_proj_kernel(x_ref, g_ref, w_ref, k_ref, v_ref):
    h = _rms(x_ref[...], g_ref[...]).astype(BF16)
    kv = _dot(h, w_ref[...].astype(BF16))
    k_ref[...] = kv[:, :SB_W].astype(BF16)
    v_ref[...] = kv[:, SB_W:].astype(BF16)


def _kv_proj(x, g, w):
    t, d = x.shape
    out = pl.BlockSpec((TM, SB_W), lambda i: (i, 0))
    return pl.pallas_call(
        _kv_proj_kernel,
        grid=(t // TM,),
        in_specs=[pl.BlockSpec((TM, d), lambda i: (i, 0)), _full((1, d)), _full(w.shape)],
        out_specs=[out, out],
        out_shape=[jax.ShapeDtypeStruct((t, SB_W), BF16)] * 2,
        compiler_params=_params("parallel"),
        name="kv_proj",
    )(x, g.reshape(1, d), w)


def _b_in_kernel(x_ref, g_ref, w_ref, qsb_ref, qmem_ref):
    h = _rms(x_ref[...], g_ref[0]).astype(BF16)
    q = _dot(h, w_ref[0].astype(BF16))
    qsb_ref[...] = (q[:, :SB_W] * (QK_SCALE * LOG2E)).astype(BF16)
    qmem_ref[...] = (q[:, SB_W:] * QK_SCALE).astype(BF16)


def _b_in(x, g, w, layer, w_layer):
    t, d = x.shape
    return pl.pallas_call(
        _b_in_kernel,
        grid=(t // TM,),
        in_specs=[pl.BlockSpec((TM, d), lambda i: (i, 0)), _layer_block((1, d), layer),
                  _layer_block(w.shape[1:], w_layer)],
        out_specs=[pl.BlockSpec((TM, SB_W), lambda i: (i, 0)), pl.BlockSpec((TM, MEM_W), lambda i: (i, 0))],
        out_shape=[jax.ShapeDtypeStruct((t, SB_W), BF16), jax.ShapeDtypeStruct((t, MEM_W), BF16)],
        compiler_params=_params("parallel"),
        name="b_in",
    )(x, g.reshape(-1, 1, d), w)


def _b_out_kernel(x_ref, ytok_ref, qmem_ref, kv_ref, wout_ref, o_ref, y_ref):
    y_ref[:, :SB_W] = ytok_ref[...]
    _memory_attention(qmem_ref[...], kv_ref.at[0], y_ref, SB_W)
    o_ref[...] = x_ref[...] + _dot(y_ref[...], wout_ref[0].astype(BF16))


def _b_out(x, y_tok, q_mem, mem_kv, w_out, layer, w_layer, batch):
    t, d = x.shape
    row = pl.BlockSpec((TM, d), lambda i: (i, 0))
    return pl.pallas_call(
        _b_out_kernel,
        grid=(t // TM,),
        in_specs=[
            row,
            pl.BlockSpec((TM, SB_W), lambda i: (i, 0)),
            pl.BlockSpec((TM, MEM_W), lambda i: (i, 0)),
            _mem_kv_block(layer, batch, t // batch),
            _layer_block(w_out.shape[1:], w_layer),
        ],
        out_specs=row,
        out_shape=jax.ShapeDtypeStruct((t, d), F32),
        scratch_shapes=[pltpu.VMEM((TM, d), BF16)],
        compiler_params=_params("parallel"),
        name="b_out",
    )(x, y_tok, q_mem, mem_kv, w_out)


def _softplus2(z):
    return jnp.maximum(z, 0.0) + jnp.log2(1.0 + jnp.exp2(-jnp.abs(z)))


def _sb_attn_kernel(q_ref, k_ref, v_ref, o_ref):
    i = pl.program_id(2)
    lane = lax.broadcasted_iota(jnp.int32, (TQ, LANES), 1)
    low = lane < HEAD_DIM
    row = lax.broadcasted_iota(jnp.int32, (TQ, TK), 0)
    col = lax.broadcasted_iota(jnp.int32, (TQ, TK), 1)
    strictly_causal = col < row
    kj = lax.broadcasted_iota(jnp.int32, (TK, TK), 0)
    ks = lax.broadcasted_iota(jnp.int32, (TK, TK), 1)
    suffix_ones = (kj >= ks).astype(BF16)

    qhs = []
    for p in range(SB_PAIRS_PER_STEP):
        q2 = q_ref[0, :, p * LANES:(p + 1) * LANES]
        qhs += [jnp.where(keep, q2, jnp.zeros_like(q2)) for keep in (low, ~low)]
    n_chains = 2 * SB_PAIRS_PER_STEP

    def scores(j, mask):
        start = pl.multiple_of(j * TK, TK)
        kts = [k_ref[0, pl.ds(start, TK), p * LANES:(p + 1) * LANES] for p in range(SB_PAIRS_PER_STEP)]
        vts = [v_ref[0, pl.ds(start, TK), p * LANES:(p + 1) * LANES] for p in range(SB_PAIRS_PER_STEP)]
        zs = [lax.dot_general(qhs[ch], kts[ch // 2], _NT, preferred_element_type=F32) for ch in range(n_chains)]
        sps = [_softplus2(z) for z in zs]
        if mask is not None:
            sps = [jnp.where(mask, sp, 0.0) for sp in sps]
        sufs = [_dot(sp.astype(BF16), suffix_ones) for sp in sps]
        return zs, sufs, [vts[ch // 2] for ch in range(n_chains)]

    def accumulate(tile, cs, accs, mask):
        zs, sufs, vts = tile
        ws = [jnp.exp2(z - suf - c) for z, suf, c in zip(zs, sufs, cs)]
        if mask is not None:
            ws = [jnp.where(mask, w, 0.0) for w in ws]
        accs = tuple(acc + _dot(w.astype(BF16), vt) for acc, w, vt in zip(accs, ws, vts))
        cs = tuple(c + suf[:, :1] for c, suf in zip(cs, sufs))
        return cs, accs

    diag = scores(i, strictly_causal)
    prev = scores(jnp.maximum(i - 1, 0), None)
    cs, accs = accumulate(diag, (jnp.zeros((TQ, 1), F32),) * n_chains,
                          (jnp.zeros((TQ, LANES), F32),) * n_chains, strictly_causal)
    cs = tuple(jnp.where(i > 0, c, SB_NO_TILE) for c in cs)
    cs, accs = accumulate(prev, cs, accs, None)

    def c_min_of(cs):
        return functools.reduce(jnp.minimum, [jnp.min(c) for c in cs])

    def keep_going(carry):
        n, c_min, _, _ = carry
        return jnp.logical_and(n < i, c_min < SB_DONE)

    def step(carry):
        n, _, cs, accs = carry
        cs, accs = accumulate(scores(i - 1 - n, None), cs, accs, None)
        return n + 1, c_min_of(cs), cs, accs

    _, _, _, accs = lax.while_loop(keep_going, step, (jnp.int32(1), c_min_of(cs), cs, accs))
    for p in range(SB_PAIRS_PER_STEP):
        o_ref[0, :, p * LANES:(p + 1) * LANES] = jnp.where(low, accs[2 * p], accs[2 * p + 1]).astype(BF16)


def _sb_attn(q, k, v):
    b, s, _ = q.shape
    width = SB_PAIRS_PER_STEP * LANES
    qspec = pl.BlockSpec((1, TQ, width), lambda bi, p, i: (bi, i, p))
    kvspec = pl.BlockSpec((1, s, width), lambda bi, p, i: (bi, 0, p))
    return pl.pallas_call(
        _sb_attn_kernel,
        grid=(b, HEAD_PAIRS_SB // SB_PAIRS_PER_STEP, s // TQ),
        in_specs=[qspec, kvspec, kvspec],
        out_specs=qspec,
        out_shape=jax.ShapeDtypeStruct((b, s, SB_W), BF16),
        compiler_params=_params("parallel", "parallel", "arbitrary"),
        name="sb_attn",
    )(q, k, v)


def kernel(x, mem, ffn1_norm, ffn1_w_gate, ffn1_w_up, ffn1_w_down, mix_norm, ffn2_norm, ffn2_w_gate, ffn2_w_up, ffn2_w_down, mem_norm, w_mem_kv, a_w_in, a_v_norm, a_w_spatial, a_b_spatial, a_w_out, kv_norm, w_kv, b_w_in, b_w_out, final_norm):
    b, s, d = x.shape
    assert d == D_MODEL and s % TM == 0 and s % TQ == 0 and TQ == TK
    t = b * s
    xt = x.reshape(t, d)
    mem_kv = _mem_kv(mem.reshape(b * N_MEM, d), mem_norm, w_mem_kv)
    mem_kv = mem_kv.reshape(DEPTH * b, N_MEM, 2 * MEM_W)
    k = v = None
    for l in range(DEPTH):
        if l == N_A:
            k, v = _kv_proj(xt, kv_norm, w_kv)
            k = k.reshape(b, s, SB_W)
            v = v.reshape(b, s, SB_W)
        xt = _ffn(xt, ffn1_norm, ffn1_w_gate, ffn1_w_up, ffn1_w_down, l)
        if l < N_A:
            xt = _mixer_a(xt, mix_norm, a_w_in, a_v_norm, a_w_spatial, a_b_spatial, mem_kv, a_w_out, l, b)
        else:
            q_sb, q_mem = _b_in(xt, mix_norm, b_w_in, l, l - N_A)
            y_tok = _sb_attn(q_sb.reshape(b, s, SB_W), k, v).reshape(t, SB_W)
            xt = _b_out(xt, y_tok, q_mem, mem_kv, b_w_out, l, l - N_A, b)
        xt = _ffn(xt, ffn2_norm, ffn2_w_gate, ffn2_w_up, ffn2_w_down, l,
                  final_gain=final_norm if l == DEPTH - 1 else None)
    return xt.reshape(b, s, d)
```

```python
import functools

import jax
import jax.numpy as jnp
from jax import lax
from jax.experimental import pallas as pl
from jax.experimental.pallas import tpu as pltpu

D_MODEL = 1024
DEPTH = 4
N_A = DEPTH // 2
CHUNK = 64
N_MEM = 256
MEM_HEADS = 4
HEAD_DIM = 64
MEM_W = MEM_HEADS * HEAD_DIM
GM_W = D_MODEL - MEM_W
GM_GROUPS = 6
GM_GC = GM_W // GM_GROUPS
GM_CHUNK = 128
SB_W = D_MODEL - MEM_W
D_FF = 2816
EPS = 1e-6

LANES = 128
HEAD_PAIRS_SB = SB_W // LANES
HEAD_PAIRS_MEM = MEM_W // LANES
QK_SCALE = HEAD_DIM ** -0.5
LOG2E = 1.4426950408889634

TM = 1024
TM_FFN = 1024
FF_CHUNK = 256
FFN_STAGE_SLOTS = 2
TQ = 256
TK = 256
SB_DONE = 158.0
SB_NO_TILE = 1e30
VMEM_LIMIT = 60 * 1024 * 1024

BF16 = jnp.bfloat16
F32 = jnp.float32
_NT = (((1,), (1,)), ((), ()))


def _params(*sem):
    return pltpu.CompilerParams(dimension_semantics=sem, vmem_limit_bytes=VMEM_LIMIT)


def _rms(x, g):
    r = lax.rsqrt(jnp.mean(x * x, axis=-1, keepdims=True) + EPS)
    return x * r * g


def _dot(a, b):
    return jnp.dot(a, b, preferred_element_type=F32)


def _full(shape):
    return pl.BlockSpec(shape, lambda *_: (0,) * len(shape))


def _layer_block(shape, layer):
    return pl.BlockSpec((1,) + tuple(shape), lambda *_: (layer,) + (0,) * len(shape),
                        pipeline_mode=pl.Buffered(1))


def _mem_kv_block(layer, batch, seq):
    steps_per_batch = seq // TM
    return pl.BlockSpec((1, N_MEM, 2 * MEM_W), lambda i: (layer * batch + i // steps_per_batch, 0, 0))


def _ffn_kernel(x_ref, g_ref, wg_hbm, wu_hbm, wd_hbm, *rest, layer, final):
    if final:
        gf_ref, o_ref, wg_v, wu_v, wd_v, stage_g, stage_u, stage_d, sem = rest
    else:
        o_ref, wg_v, wu_v, wd_v, stage_g, stage_u, stage_d, sem = rest
    n_chunks = D_FF // FF_CHUNK

    def chunk_copies(c):
        cols = pl.ds(c * FF_CHUNK, FF_CHUNK)
        slot = c % FFN_STAGE_SLOTS
        return (pltpu.make_async_copy(wg_hbm.at[layer, :, cols], stage_g.at[slot], sem.at[0, slot]),
                pltpu.make_async_copy(wu_hbm.at[layer, :, cols], stage_u.at[slot], sem.at[1, slot]),
                pltpu.make_async_copy(wd_hbm.at[layer, cols, :], stage_d.at[slot], sem.at[2, slot]))

    def body(stream_weights):
        if stream_weights:
            for c in range(FFN_STAGE_SLOTS):
                for copy in chunk_copies(c):
                    copy.start()
        h = _rms(x_ref[...], g_ref[0]).astype(BF16)
        for c in range(n_chunks):
            cols = slice(c * FF_CHUNK, (c + 1) * FF_CHUNK)
            if stream_weights:
                slot = c % FFN_STAGE_SLOTS
                for copy in chunk_copies(c):
                    copy.wait()
                wg_v[:, cols] = stage_g[slot].astype(BF16)
                wu_v[:, cols] = stage_u[slot].astype(BF16)
                wd_v[cols, :] = stage_d[slot].astype(BF16)
                if c + FFN_STAGE_SLOTS < n_chunks:
                    for copy in chunk_copies(c + FFN_STAGE_SLOTS):
                        copy.start()
            gate = _dot(h, wg_v[:, cols])
            up = _dot(h, wu_v[:, cols])
            act = (gate * jax.nn.sigmoid(gate) * up).astype(BF16)
            part = _dot(act, wd_v[cols, :])
            if c == 0:
                o_ref[...] = part
            else:
                o_ref[...] += part
        y = x_ref[...] + 0.5 * o_ref[...]
        if final:
            y = _rms(y, gf_ref[...])
        o_ref[...] = y

    first = pl.program_id(0) == 0
    pl.when(first)(functools.partial(body, True))
    pl.when(jnp.logical_not(first))(functools.partial(body, False))


def _ffn(x, g, wg, wu, wd, layer, final_gain=None):
    t, d = x.shape
    final = final_gain is not None
    row = pl.BlockSpec((TM_FFN, d), lambda i: (i, 0))
    hbm = pl.BlockSpec(memory_space=pl.ANY)
    in_specs = [row, _layer_block((1, d), layer), hbm, hbm, hbm]
    args = [x, g.reshape(-1, 1, d), wg, wu, wd]
    if final:
        in_specs.append(_full((1, d)))
        args.append(final_gain.reshape(1, d))
    return pl.pallas_call(
        functools.partial(_ffn_kernel, layer=layer, final=final),
        grid=(t // TM_FFN,),
        in_specs=in_specs,
        out_specs=row,
        out_shape=jax.ShapeDtypeStruct((t, d), F32),
        scratch_shapes=[
            pltpu.VMEM((d, D_FF), BF16), pltpu.VMEM((d, D_FF), BF16), pltpu.VMEM((D_FF, d), BF16),
            pltpu.VMEM((FFN_STAGE_SLOTS, d, FF_CHUNK), F32), pltpu.VMEM((FFN_STAGE_SLOTS, d, FF_CHUNK), F32),
            pltpu.VMEM((FFN_STAGE_SLOTS, FF_CHUNK, d), F32),
            pltpu.SemaphoreType.DMA((3, FFN_STAGE_SLOTS)),
        ],
        compiler_params=_params("arbitrary"),
        name="ffn_final" if final else "ffn",
    )(*args)


def _mem_kv_kernel(mem_ref, g_ref, w_ref, o_ref):
    h = _rms(mem_ref[...], g_ref[...]).astype(BF16)
    o_ref[0] = _dot(h, w_ref[0].astype(BF16)).astype(BF16)


def _mem_kv(mem, g, w):
    n, d = mem.shape
    depth, _, width = w.shape
    return pl.pallas_call(
        _mem_kv_kernel,
        grid=(depth,),
        in_specs=[_full((n, d)), _full((1, d)), pl.BlockSpec((1, d, width), lambda l: (l, 0, 0))],
        out_specs=pl.BlockSpec((1, n, width), lambda l: (l, 0, 0)),
        out_shape=jax.ShapeDtypeStruct((depth, n, width), BF16),
        compiler_params=_params("parallel"),
        name="mem_kv",
    )(mem, g.reshape(1, d), w)


def _memory_attention(q_mem, kv_ref):
    rows = q_mem.shape[0]
    lane = lax.broadcasted_iota(jnp.int32, (rows, LANES), 1)
    low = lane < HEAD_DIM
    pairs = []
    for p in range(HEAD_PAIRS_MEM):
        cols = slice(p * LANES, (p + 1) * LANES)
        q2 = q_mem[:, cols]
        k2 = kv_ref[:, cols]
        v2 = kv_ref[:, MEM_W + p * LANES:MEM_W + (p + 1) * LANES]
        outs = []
        for keep in (low, ~low):
            s = lax.dot_general(jnp.where(keep, q2, jnp.zeros_like(q2)), k2, _NT,
                                preferred_element_type=F32)
            e = jnp.exp(s - jnp.max(s, axis=-1, keepdims=True))
            inv = 1.0 / jnp.sum(e, axis=-1, keepdims=True)
            outs.append(_dot(e.astype(BF16), v2) * inv)
        pairs.append(jnp.where(low, outs[0], outs[1]).astype(BF16))
    return jnp.concatenate(pairs, axis=1)


def _mixer_a_kernel(x_ref, g_ref, win_ref, vg_ref, ws_ref, bs_ref, kv_ref, wout_ref, o_ref, y_ref):
    x = x_ref[...]
    h = _rms(x, g_ref[0]).astype(BF16)
    proj = _dot(h, win_ref[0].astype(BF16))
    u = jax.nn.gelu(proj[:, :GM_W])
    v = _rms(jax.nn.gelu(proj[:, GM_W:2 * GM_W]), vg_ref[0]).astype(BF16)
    pos_t = lax.broadcasted_iota(jnp.int32, (GM_CHUNK, GM_CHUNK), 0) // CHUNK
    pos_s = lax.broadcasted_iota(jnp.int32, (GM_CHUNK, GM_CHUNK), 1) // CHUNK
    chunk_causal = pos_s <= pos_t
    for gi in range(GM_GROUPS):
        cols = slice(gi * GM_GC, (gi + 1) * GM_GC)
        w = jnp.where(chunk_causal, ws_ref[0, gi], 0.0).astype(BF16)
        bias = bs_ref[0, gi]
        for n in range(TM // GM_CHUNK):
            rows = slice(n * GM_CHUNK, (n + 1) * GM_CHUNK)
            mixed = _dot(w, v[rows, cols]) + bias
            y_ref[rows, cols] = (u[rows, cols] * mixed).astype(BF16)
    q_mem = (proj[:, 2 * GM_W:] * QK_SCALE).astype(BF16)
    y_ref[:, GM_W:] = _memory_attention(q_mem, kv_ref.at[0])
    o_ref[...] = x + _dot(y_ref[...], wout_ref[0].astype(BF16))


def _mixer_a(x, g, w_in, v_gain, w_sp, b_sp, mem_kv, w_out, layer, batch):
    t, d = x.shape
    row = pl.BlockSpec((TM, d), lambda i: (i, 0))
    return pl.pallas_call(
        _mixer_a_kernel,
        grid=(t // TM,),
        in_specs=[
            row, _layer_block((1, d), layer), _layer_block(w_in.shape[1:], layer),
            _layer_block((1, GM_W), layer), _layer_block(w_sp.shape[1:], layer),
            _layer_block((GM_GROUPS, GM_CHUNK, 1), layer),
            _mem_kv_block(layer, batch, t // batch),
            _layer_block(w_out.shape[1:], layer),
        ],
        out_specs=row,
        out_shape=jax.ShapeDtypeStruct((t, d), F32),
        scratch_shapes=[pltpu.VMEM((TM, d), BF16)],
        compiler_params=_params("parallel"),
        name="mixer_a",
    )(x, g.reshape(-1, 1, d), w_in, v_gain.reshape(-1, 1, GM_W), w_sp,
      b_sp.reshape(-1, GM_GROUPS, GM_CHUNK, 1), mem_kv, w_out)


def _kv_proj_kernel(x_ref, g_ref, w_ref, k_ref, v_ref):
    h = _rms(x_ref[...], g_ref[...]).astype(BF16)
    kv = _dot(h, w_ref[...].astype(BF16))
    k_ref[...] = kv[:, :SB_W].astype(BF16)
    v_ref[...] = kv[:, SB_W:].astype(BF16)


def _kv_proj(x, g, w):
    t, d = x.shape
    out = pl.BlockSpec((TM, SB_W), lambda i: (i, 0))
    return pl.pallas_call(
        _kv_proj_kernel,
        grid=(t // TM,),
        in_specs=[pl.BlockSpec((TM, d), lambda i: (i, 0)), _full((1, d)), _full(w.shape)],
        out_specs=[out, out],
        out_shape=[jax.ShapeDtypeStruct((t, SB_W), BF16)] * 2,
        compiler_params=_params("parallel"),
        name="kv_proj",
    )(x, g.reshape(1, d), w)


def _softplus2(z):
    return jnp.maximum(z, 0.0) + jnp.log2(1.0 + jnp.exp2(-jnp.abs(z)))


def _sb_sweep(i, q_tile, k_ref, v_ref, between_stages):
    lane = lax.broadcasted_iota(jnp.int32, (TQ, LANES), 1)
    low = lane < HEAD_DIM
    kj = lax.broadcasted_iota(jnp.int32, (TK, TK), 0)
    ks = lax.broadcasted_iota(jnp.int32, (TK, TK), 1)
    suffix_ones = (kj >= ks).astype(BF16)

    qhs = []
    for p in range(HEAD_PAIRS_SB):
        q2 = q_tile[:, p * LANES:(p + 1) * LANES]
        qhs += [jnp.where(keep, q2, jnp.zeros_like(q2)) for keep in (low, ~low)]
    n_chains = 2 * HEAD_PAIRS_SB

    half = TQ // 2
    full_pieces = [(0, TQ, TK, None)]
    diag_pieces = []
    for r0, r1, nk in ((0, half, half), (half, TQ, TK)):
        row = lax.broadcasted_iota(jnp.int32, (r1 - r0, nk), 0) + r0
        col = lax.broadcasted_iota(jnp.int32, (r1 - r0, nk), 1)
        diag_pieces.append((r0, r1, nk, col < row))

    def scores(j, pieces):
        start = pl.multiple_of(j * TK, TK)
        kts = [k_ref[pl.ds(start, TK), p * LANES:(p + 1) * LANES] for p in range(HEAD_PAIRS_SB)]
        vts = [v_ref[pl.ds(start, TK), p * LANES:(p + 1) * LANES] for p in range(HEAD_PAIRS_SB)]
        out = []
        for r0, r1, nk, mask in pieces:
            zs = [lax.dot_general(qhs[ch][r0:r1], kts[ch // 2][:nk], _NT, preferred_element_type=F32)
                  for ch in range(n_chains)]
            sps = [_softplus2(z) for z in zs]
            if mask is not None:
                sps = [jnp.where(mask, sp, 0.0) for sp in sps]
            sufs = [_dot(sp.astype(BF16), suffix_ones[:nk, :nk]) for sp in sps]
            out.append((zs, sufs, [vts[ch // 2][:nk] for ch in range(n_chains)]))
        return out

    def accumulate(tile, pieces, cs, accs):
        new_cs, new_accs = [[] for _ in range(n_chains)], [[] for _ in range(n_chains)]
        for (r0, r1, nk, mask), (zs, sufs, vts) in zip(pieces, tile):
            ws = [jnp.exp2(z - suf - c[r0:r1]) for z, suf, c in zip(zs, sufs, cs)]
            if mask is not None:
                ws = [jnp.where(mask, w, 0.0) for w in ws]
            for ch in range(n_chains):
                new_accs[ch].append(_dot(ws[ch].astype(BF16), vts[ch]))
                new_cs[ch].append(sufs[ch][:, :1])
        accs = tuple(acc + jnp.concatenate(parts, axis=0) for acc, parts in zip(accs, new_accs))
        cs = tuple(c + jnp.concatenate(parts, axis=0) for c, parts in zip(cs, new_cs))
        return cs, accs

    diag = scores(i, diag_pieces)
    prev = scores(jnp.maximum(i - 1, 0), full_pieces)
    between_stages()
    cs, accs = accumulate(diag, diag_pieces, (jnp.zeros((TQ, 1), F32),) * n_chains,
                          (jnp.zeros((TQ, LANES), F32),) * n_chains)
    cs = tuple(jnp.where(i > 0, c, SB_NO_TILE) for c in cs)
    cs, accs = accumulate(prev, full_pieces, cs, accs)

    def c_min_of(cs):
        return functools.reduce(jnp.minimum, [jnp.min(c) for c in cs])

    def keep_going(carry):
        n, c_min, _, _ = carry
        return jnp.logical_and(n < i, c_min < SB_DONE)

    def step(carry):
        n, _, cs, accs = carry
        cs, accs = accumulate(scores(i - 1 - n, full_pieces), full_pieces, cs, accs)
        return n + 1, c_min_of(cs), cs, accs

    _, _, _, accs = lax.while_loop(keep_going, step, (jnp.int32(1), c_min_of(cs), cs, accs))
    return jnp.concatenate([jnp.where(low, accs[2 * p], accs[2 * p + 1]).astype(BF16)
                            for p in range(HEAD_PAIRS_SB)], axis=1)


def _mixer_b_kernel(xa_ref, xc_ref, g_ref, win_ref, k_ref, v_ref, kv_ref, wout_ref, o_ref,
                    q_scr, qm_scr, y_scr):
    s = pl.program_id(1)
    n = pl.num_programs(1) - 1

    def in_proj(x):
        h = _rms(x, g_ref[0]).astype(BF16)
        q = _dot(h, win_ref[0])
        return (q[:, :SB_W] * (QK_SCALE * LOG2E)).astype(BF16), (q[:, SB_W:] * QK_SCALE).astype(BF16)

    def finish(y_tok, q_mem):
        mem = _memory_attention(q_mem, kv_ref.at[0])
        o_ref[...] = xc_ref[...] + _dot(y_tok, wout_ref[0, :SB_W, :]) + _dot(mem, wout_ref[0, SB_W:, :])

    slot = s % 2
    other = 1 - slot

    @pl.when(s == 0)
    def _():
        q_scr[0], qm_scr[0] = in_proj(xc_ref[...])
        qm_scr[1] = jnp.zeros((TQ, MEM_W), BF16)
        y_scr[1] = jnp.zeros((TQ, SB_W), BF16)

    @pl.when(s < n)
    def _():
        y_prev = y_scr[other]
        qm_prev = qm_scr[other]

        def project_next():
            q_scr[other], qm_scr[other] = in_proj(xa_ref[...])
            finish(y_prev, qm_prev)

        y_scr[slot] = _sb_sweep(s, q_scr[slot], k_ref.at[0], v_ref.at[0], project_next)

    @pl.when(s == n)
    def _():
        finish(y_scr[other], qm_scr[other])


def _mixer_b(x, g, w_in, k, v, mem_kv, w_out, layer, w_layer):
    t, d = x.shape
    b, seq, _ = k.shape
    n = seq // TQ
    xa = pl.BlockSpec((TQ, d), lambda bi, s: (bi * n + jnp.minimum(s + 1, n - 1), 0))
    xc = pl.BlockSpec((TQ, d), lambda bi, s: (bi * n + jnp.maximum(s - 1, 0), 0))
    kvspec = pl.BlockSpec((1, seq, SB_W), lambda bi, s: (bi, 0, 0), pipeline_mode=pl.Buffered(1))
    return pl.pallas_call(
        _mixer_b_kernel,
        grid=(b, n + 1),
        in_specs=[
            xa, xc, _layer_block((1, d), layer), _layer_block(w_in.shape[1:], w_layer), kvspec, kvspec,
            pl.BlockSpec((1, N_MEM, 2 * MEM_W), lambda bi, s: (layer * b + bi, 0, 0)),
            _layer_block(w_out.shape[1:], w_layer),
        ],
        out_specs=xc,
        out_shape=jax.ShapeDtypeStruct((t, d), F32),
        scratch_shapes=[
            pltpu.VMEM((2, TQ, SB_W), BF16), pltpu.VMEM((2, TQ, MEM_W), BF16), pltpu.VMEM((2, TQ, SB_W), BF16),
        ],
        compiler_params=_params("arbitrary", "arbitrary"),
        name="mixer_b",
    )(x, x, g.reshape(-1, 1, d), w_in, k, v, mem_kv, w_out)


def kernel(x, mem, ffn1_norm, ffn1_w_gate, ffn1_w_up, ffn1_w_down, mix_norm, ffn2_norm, ffn2_w_gate, ffn2_w_up, ffn2_w_down, mem_norm, w_mem_kv, a_w_in, a_v_norm, a_w_spatial, a_b_spatial, a_w_out, kv_norm, w_kv, b_w_in, b_w_out, final_norm):
    b, s, d = x.shape
    assert d == D_MODEL and s % TM == 0 and s % TM_FFN == 0 and s % TQ == 0 and TQ == TK
    t = b * s
    xt = x.reshape(t, d)
    mem_kv = _mem_kv(mem.reshape(b * N_MEM, d), mem_norm, w_mem_kv)
    mem_kv = mem_kv.reshape(DEPTH * b, N_MEM, 2 * MEM_W)
    k = v = None
    for l in range(DEPTH):
        if l == N_A:
            k, v = _kv_proj(xt, kv_norm, w_kv)
            k = k.reshape(b, s, SB_W)
            v = v.reshape(b, s, SB_W)
        xt = _ffn(xt, ffn1_norm, ffn1_w_gate, ffn1_w_up, ffn1_w_down, l)
        if l < N_A:
            xt = _mixer_a(xt, mix_norm, a_w_in, a_v_norm, a_w_spatial, a_b_spatial, mem_kv, a_w_out, l, b)
        else:
            xt = _mixer_b(xt, mix_norm, b_w_in.astype(BF16), k, v, mem_kv, b_w_out.astype(BF16), l, l - N_A)
        xt = _ffn(xt, ffn2_norm, ffn2_w_gate, ffn2_w_up, ffn2_w_down, l,
                  final_gain=final_norm if l == DEPTH - 1 else None)
    return xt.reshape(b, s, d)
```

```python
import functools

import jax
import jax.numpy as jnp
from jax import lax
from jax.experimental import pallas as pl
from jax.experimental.pallas import tpu as pltpu

D_MODEL = 1024
DEPTH = 4
N_A = DEPTH // 2
CHUNK = 64
N_MEM = 256
MEM_HEADS = 4
HEAD_DIM = 64
MEM_W = MEM_HEADS * HEAD_DIM
GM_W = D_MODEL - MEM_W
GM_GROUPS = 6
GM_GC = GM_W // GM_GROUPS
GM_CHUNK = 128
SB_W = D_MODEL - MEM_W
D_FF = 2816
EPS = 1e-6

LANES = 128
HEAD_PAIRS_SB = SB_W // LANES
HEAD_PAIRS_MEM = MEM_W // LANES
QK_SCALE = HEAD_DIM ** -0.5
LOG2E = 1.4426950408889634

TM = 1024
TM_FFN = 1024
FF_CHUNK = 256
TQ = 256
TK = 256
SB_DONE = 158.0
SB_NO_TILE = 1e30
VMEM_LIMIT = 60 * 1024 * 1024

BF16 = jnp.bfloat16
F32 = jnp.float32
_NT = (((1,), (1,)), ((), ()))


def _params(*sem):
    return pltpu.CompilerParams(dimension_semantics=sem, vmem_limit_bytes=VMEM_LIMIT)


def _rms(x, g):
    r = lax.rsqrt(jnp.mean(x * x, axis=-1, keepdims=True) + EPS)
    return x * r * g


def _dot(a, b):
    return jnp.dot(a, b, preferred_element_type=F32)


def _full(shape):
    return pl.BlockSpec(shape, lambda *_: (0,) * len(shape))


def _layer_block(shape, layer):
    return pl.BlockSpec((1,) + tuple(shape), lambda *_: (layer,) + (0,) * len(shape),
                        pipeline_mode=pl.Buffered(1))


def _mem_kv_block(layer, batch, seq):
    steps_per_batch = seq // TM
    return pl.BlockSpec((1, N_MEM, 2 * MEM_W), lambda i: (layer * batch + i // steps_per_batch, 0, 0))


def _ffn_kernel(x_ref, g_ref, wg_ref, wu_ref, wd_ref, *rest, final):
    if final:
        gf_ref, o_ref = rest
    else:
        o_ref, = rest
    h = _rms(x_ref[...], g_ref[0]).astype(BF16)
    for c in range(D_FF // FF_CHUNK):
        cols = slice(c * FF_CHUNK, (c + 1) * FF_CHUNK)
        gate = _dot(h, wg_ref[0, :, cols].astype(BF16))
        up = _dot(h, wu_ref[0, :, cols].astype(BF16))
        act = (gate * jax.nn.sigmoid(gate) * up).astype(BF16)
        part = _dot(act, wd_ref[0, cols, :].astype(BF16))
        if c == 0:
            o_ref[...] = part
        else:
            o_ref[...] += part
    y = x_ref[...] + 0.5 * o_ref[...]
    if final:
        y = _rms(y, gf_ref[...])
    o_ref[...] = y


def _ffn(x, g, wg, wu, wd, layer, final_gain=None):
    t, d = x.shape
    final = final_gain is not None
    row = pl.BlockSpec((TM_FFN, d), lambda i: (i, 0))
    in_specs = [row, _layer_block((1, d), layer), _layer_block((d, D_FF), layer),
                _layer_block((d, D_FF), layer), _layer_block((D_FF, d), layer)]
    args = [x, g.reshape(-1, 1, d), wg, wu, wd]
    if final:
        in_specs.append(_full((1, d)))
        args.append(final_gain.reshape(1, d))
    return pl.pallas_call(
        functools.partial(_ffn_kernel, final=final),
        grid=(t // TM_FFN,),
        in_specs=in_specs,
        out_specs=row,
        out_shape=jax.ShapeDtypeStruct((t, d), F32),
        compiler_params=_params("parallel"),
        name="ffn_final" if final else "ffn",
    )(*args)


def _mem_kv_kernel(mem_ref, g_ref, w_ref, o_ref):
    h = _rms(mem_ref[...], g_ref[...]).astype(BF16)
    o_ref[0] = _dot(h, w_ref[0].astype(BF16)).astype(BF16)


def _mem_kv(mem, g, w):
    n, d = mem.shape
    depth, _, width = w.shape
    return pl.pallas_call(
        _mem_kv_kernel,
        grid=(depth,),
        in_specs=[_full((n, d)), _full((1, d)), pl.BlockSpec((1, d, width), lambda l: (l, 0, 0))],
        out_specs=pl.BlockSpec((1, n, width), lambda l: (l, 0, 0)),
        out_shape=jax.ShapeDtypeStruct((depth, n, width), BF16),
        compiler_params=_params("parallel"),
        name="mem_kv",
    )(mem, g.reshape(1, d), w)


def _memory_attention(q_mem, kv_ref):
    rows = q_mem.shape[0]
    lane = lax.broadcasted_iota(jnp.int32, (rows, LANES), 1)
    low = lane < HEAD_DIM
    pairs = []
    for p in range(HEAD_PAIRS_MEM):
        cols = slice(p * LANES, (p + 1) * LANES)
        q2 = q_mem[:, cols]
        k2 = kv_ref[:, cols]
        v2 = kv_ref[:, MEM_W + p * LANES:MEM_W + (p + 1) * LANES]
        outs = []
        for keep in (low, ~low):
            s = lax.dot_general(jnp.where(keep, q2, jnp.zeros_like(q2)), k2, _NT,
                                preferred_element_type=F32)
            e = jnp.exp(s - jnp.max(s, axis=-1, keepdims=True))
            inv = 1.0 / jnp.sum(e, axis=-1, keepdims=True)
            outs.append(_dot(e.astype(BF16), v2) * inv)
        pairs.append(jnp.where(low, outs[0], outs[1]).astype(BF16))
    return jnp.concatenate(pairs, axis=1)


def _mixer_a_kernel(x_ref, g_ref, win_ref, vg_ref, ws_ref, bs_ref, kv_ref, wout_ref, o_ref, y_ref):
    x = x_ref[...]
    h = _rms(x, g_ref[0]).astype(BF16)
    proj = _dot(h, win_ref[0].astype(BF16))
    u = jax.nn.gelu(proj[:, :GM_W])
    v = _rms(jax.nn.gelu(proj[:, GM_W:2 * GM_W]), vg_ref[0]).astype(BF16)
    pos_t = lax.broadcasted_iota(jnp.int32, (GM_CHUNK, GM_CHUNK), 0) // CHUNK
    pos_s = lax.broadcasted_iota(jnp.int32, (GM_CHUNK, GM_CHUNK), 1) // CHUNK
    chunk_causal = pos_s <= pos_t
    for gi in range(GM_GROUPS):
        cols = slice(gi * GM_GC, (gi + 1) * GM_GC)
        w = jnp.where(chunk_causal, ws_ref[0, gi], 0.0).astype(BF16)
        bias = bs_ref[0, gi]
        for n in range(TM // GM_CHUNK):
            rows = slice(n * GM_CHUNK, (n + 1) * GM_CHUNK)
            mixed = _dot(w, v[rows, cols]) + bias
            y_ref[rows, cols] = (u[rows, cols] * mixed).astype(BF16)
    q_mem = (proj[:, 2 * GM_W:] * QK_SCALE).astype(BF16)
    y_ref[:, GM_W:] = _memory_attention(q_mem, kv_ref.at[0])
    o_ref[...] = x + _dot(y_ref[...], wout_ref[0].astype(BF16))


def _mixer_a(x, g, w_in, v_gain, w_sp, b_sp, mem_kv, w_out, layer, batch):
    t, d = x.shape
    row = pl.BlockSpec((TM, d), lambda i: (i, 0))
    return pl.pallas_call(
        _mixer_a_kernel,
        grid=(t // TM,),
        in_specs=[
            row, _layer_block((1, d), layer), _layer_block(w_in.shape[1:], layer),
            _layer_block((1, GM_W), layer), _layer_block(w_sp.shape[1:], layer),
            _layer_block((GM_GROUPS, GM_CHUNK, 1), layer),
            _mem_kv_block(layer, batch, t // batch),
            _layer_block(w_out.shape[1:], layer),
        ],
        out_specs=row,
        out_shape=jax.ShapeDtypeStruct((t, d), F32),
        scratch_shapes=[pltpu.VMEM((TM, d), BF16)],
        compiler_params=_params("parallel"),
        name="mixer_a",
    )(x, g.reshape(-1, 1, d), w_in, v_gain.reshape(-1, 1, GM_W), w_sp,
      b_sp.reshape(-1, GM_GROUPS, GM_CHUNK, 1), mem_kv, w_out)


def _kv_proj_kernel(x_ref, g_ref, w_ref, k_ref, v_ref):
    h = _rms(x_ref[...], g_ref[...]).astype(BF16)
    kv = _dot(h, w_ref[...].astype(BF16))
    k_ref[...] = kv[:, :SB_W].astype(BF16)
    v_ref[...] = kv[:, SB_W:].astype(BF16)


def _kv_proj(x, g, w):
    t, d = x.shape
    out = pl.BlockSpec((TM, SB_W), lambda i: (i, 0))
    return pl.pallas_call(
        _kv_proj_kernel,
        grid=(t // TM,),
        in_specs=[pl.BlockSpec((TM, d), lambda i: (i, 0)), _full((1, d)), _full(w.shape)],
        out_specs=[out, out],
        out_shape=[jax.ShapeDtypeStruct((t, SB_W), BF16)] * 2,
        compiler_params=_params("parallel"),
        name="kv_proj",
    )(x, g.reshape(1, d), w)


def _softplus2(z):
    return jnp.maximum(z, 0.0) + jnp.log2(1.0 + jnp.exp2(-jnp.abs(z)))


def _sb_sweep(i, q_tile, k_ref, v_ref, fillers):
    fillers = list(fillers)

    def emit_filler():
        if fillers:
            fillers.pop(0)()

    lane = lax.broadcasted_iota(jnp.int32, (TQ, LANES), 1)
    low = lane < HEAD_DIM
    kj = lax.broadcasted_iota(jnp.int32, (TK, TK), 0)
    ks = lax.broadcasted_iota(jnp.int32, (TK, TK), 1)
    suffix_ones = (kj >= ks).astype(BF16)

    qhs = []
    for p in range(HEAD_PAIRS_SB):
        q2 = q_tile[:, p * LANES:(p + 1) * LANES]
        qhs += [jnp.where(keep, q2, jnp.zeros_like(q2)) for keep in (low, ~low)]
    n_chains = 2 * HEAD_PAIRS_SB

    half = TQ // 2
    full_pieces = [(0, TQ, TK, None)]
    diag_pieces = []
    for r0, r1, nk in ((0, half, half), (half, TQ, TK)):
        row = lax.broadcasted_iota(jnp.int32, (r1 - r0, nk), 0) + r0
        col = lax.broadcasted_iota(jnp.int32, (r1 - r0, nk), 1)
        diag_pieces.append((r0, r1, nk, col < row))

    def raw_scores(j, pieces):
        start = pl.multiple_of(j * TK, TK)
        kts = [k_ref[pl.ds(start, TK), p * LANES:(p + 1) * LANES] for p in range(HEAD_PAIRS_SB)]
        vts = [v_ref[pl.ds(start, TK), p * LANES:(p + 1) * LANES] for p in range(HEAD_PAIRS_SB)]
        out = []
        for r0, r1, nk, _ in pieces:
            zs = [lax.dot_general(qhs[ch][r0:r1], kts[ch // 2][:nk], _NT, preferred_element_type=F32)
                  for ch in range(n_chains)]
            out.append((zs, [vts[ch // 2][:nk] for ch in range(n_chains)]))
            emit_filler()
        return out

    def with_suffix_sums(raw, pieces):
        out = []
        for (r0, r1, nk, mask), (zs, vts) in zip(pieces, raw):
            sps = [_softplus2(z) for z in zs]
            if mask is not None:
                sps = [jnp.where(mask, sp, 0.0) for sp in sps]
            sufs = [_dot(sp.astype(BF16), suffix_ones[:nk, :nk]) for sp in sps]
            out.append((zs, sufs, vts))
        return out

    def scores(j, pieces):
        return with_suffix_sums(raw_scores(j, pieces), pieces)

    def accumulate(tile, pieces, cs, accs):
        new_cs, new_accs = [[] for _ in range(n_chains)], [[] for _ in range(n_chains)]
        for (r0, r1, nk, mask), (zs, sufs, vts) in zip(pieces, tile):
            ws = [jnp.exp2(z - suf - c[r0:r1]) for z, suf, c in zip(zs, sufs, cs)]
            if mask is not None:
                ws = [jnp.where(mask, w, 0.0) for w in ws]
            for ch in range(n_chains):
                new_accs[ch].append(_dot(ws[ch].astype(BF16), vts[ch]))
                new_cs[ch].append(sufs[ch][:, :1])
            emit_filler()
        accs = tuple(acc + jnp.concatenate(parts, axis=0) for acc, parts in zip(accs, new_accs))
        cs = tuple(c + jnp.concatenate(parts, axis=0) for c, parts in zip(cs, new_cs))
        return cs, accs

    diag = scores(i, diag_pieces)
    prev_raw = raw_scores(jnp.maximum(i - 1, 0), full_pieces)
    cs, accs = accumulate(diag, diag_pieces, (jnp.zeros((TQ, 1), F32),) * n_chains,
                          (jnp.zeros((TQ, LANES), F32),) * n_chains)
    prev = with_suffix_sums(prev_raw, full_pieces)
    cs = tuple(jnp.where(i > 0, c, SB_NO_TILE) for c in cs)
    cs, accs = accumulate(prev, full_pieces, cs, accs)
    while fillers:
        emit_filler()

    def c_min_of(cs):
        return functools.reduce(jnp.minimum, [jnp.min(c) for c in cs])

    def keep_going(carry):
        n, c_min, _, _ = carry
        return jnp.logical_and(n < i, c_min < SB_DONE)

    def step(carry):
        n, _, cs, accs = carry
        cs, accs = accumulate(scores(i - 1 - n, full_pieces), full_pieces, cs, accs)
        return n + 1, c_min_of(cs), cs, accs

    _, _, _, accs = lax.while_loop(keep_going, step, (jnp.int32(1), c_min_of(cs), cs, accs))
    return jnp.concatenate([jnp.where(low, accs[2 * p], accs[2 * p + 1]).astype(BF16)
                            for p in range(HEAD_PAIRS_SB)], axis=1)


def _mixer_b_kernel(xa_ref, xc_ref, g_ref, win_ref, k_ref, v_ref, kv_ref, wout_ref, o_ref,
                    q_scr, qm_scr, y_scr):
    s = pl.program_id(1)
    n = pl.num_programs(1) - 1

    def in_proj(x):
        h = _rms(x, g_ref[0]).astype(BF16)
        q = _dot(h, win_ref[0])
        return (q[:, :SB_W] * (QK_SCALE * LOG2E)).astype(BF16), (q[:, SB_W:] * QK_SCALE).astype(BF16)

    def finish(y_tok, q_mem):
        mem = _memory_attention(q_mem, kv_ref.at[0])
        o_ref[...] = xc_ref[...] + _dot(y_tok, wout_ref[0, :SB_W, :]) + _dot(mem, wout_ref[0, SB_W:, :])

    slot = s % 2
    other = 1 - slot

    @pl.when(s == 0)
    def _():
        q_scr[0], qm_scr[0] = in_proj(xc_ref[...])
        qm_scr[1] = jnp.zeros((TQ, MEM_W), BF16)
        y_scr[1] = jnp.zeros((TQ, SB_W), BF16)

    @pl.when(s < n)
    def _():
        y_prev = y_scr[other]
        qm_prev = qm_scr[other]
        half = D_MODEL // 2
        held = {}

        def norm_next():
            held["h"] = _rms(xa_ref[...], g_ref[0]).astype(BF16)

        def project_sb():
            q = _dot(held["h"], win_ref[0, :, :half])
            q_scr[other, :, :half] = (q * (QK_SCALE * LOG2E)).astype(BF16)

        def project_rest():
            q = _dot(held["h"], win_ref[0, :, half:])
            q_scr[other, :, half:] = (q[:, :SB_W - half] * (QK_SCALE * LOG2E)).astype(BF16)
            qm_scr[other] = (q[:, SB_W - half:] * QK_SCALE).astype(BF16)

        def attend_memory():
            held["mem"] = _memory_attention(qm_prev, kv_ref.at[0])

        def finish_cols(c0, c1):
            o_ref[:, c0:c1] = (xc_ref[:, c0:c1] + _dot(y_prev, wout_ref[0, :SB_W, c0:c1])
                               + _dot(held["mem"], wout_ref[0, SB_W:, c0:c1]))

        fillers = [norm_next, project_sb, project_rest, attend_memory,
                   functools.partial(finish_cols, 0, half), functools.partial(finish_cols, half, D_MODEL)]
        y_scr[slot] = _sb_sweep(s, q_scr[slot], k_ref.at[0], v_ref.at[0], fillers)

    @pl.when(s == n)
    def _():
        finish(y_scr[other], qm_scr[other])


def _mixer_b(x, g, w_in, k, v, mem_kv, w_out, layer, w_layer):
    t, d = x.shape
    b, seq, _ = k.shape
    n = seq // TQ
    xa = pl.BlockSpec((TQ, d), lambda bi, s: (bi * n + jnp.minimum(s + 1, n - 1), 0))
    xc = pl.BlockSpec((TQ, d), lambda bi, s: (bi * n + jnp.maximum(s - 1, 0), 0))
    kvspec = pl.BlockSpec((1, seq, SB_W), lambda bi, s: (bi, 0, 0), pipeline_mode=pl.Buffered(1))
    return pl.pallas_call(
        _mixer_b_kernel,
        grid=(b, n + 1),
        in_specs=[
            xa, xc, _layer_block((1, d), layer), _layer_block(w_in.shape[1:], w_layer), kvspec, kvspec,
            pl.BlockSpec((1, N_MEM, 2 * MEM_W), lambda bi, s: (layer * b + bi, 0, 0)),
            _layer_block(w_out.shape[1:], w_layer),
        ],
        out_specs=xc,
        out_shape=jax.ShapeDtypeStruct((t, d), F32),
        scratch_shapes=[
            pltpu.VMEM((2, TQ, SB_W), BF16), pltpu.VMEM((2, TQ, MEM_W), BF16), pltpu.VMEM((2, TQ, SB_W), BF16),
        ],
        compiler_params=_params("arbitrary", "arbitrary"),
        name="mixer_b",
    )(x, x, g.reshape(-1, 1, d), w_in, k, v, mem_kv, w_out)


def kernel(x, mem, ffn1_norm, ffn1_w_gate, ffn1_w_up, ffn1_w_down, mix_norm, ffn2_norm, ffn2_w_gate, ffn2_w_up, ffn2_w_down, mem_norm, w_mem_kv, a_w_in, a_v_norm, a_w_spatial, a_b_spatial, a_w_out, kv_norm, w_kv, b_w_in, b_w_out, final_norm):
    b, s, d = x.shape
    assert d == D_MODEL and s % TM == 0 and s % TM_FFN == 0 and s % TQ == 0 and TQ == TK
    t = b * s
    xt = x.reshape(t, d)
    mem_kv = _mem_kv(mem.reshape(b * N_MEM, d), mem_norm, w_mem_kv)
    mem_kv = mem_kv.reshape(DEPTH * b, N_MEM, 2 * MEM_W)
    k = v = None
    for l in range(DEPTH):
        if l == N_A:
            k, v = _kv_proj(xt, kv_norm, w_kv)
            k = k.reshape(b, s, SB_W)
            v = v.reshape(b, s, SB_W)
        xt = _ffn(xt, ffn1_norm, ffn1_w_gate, ffn1_w_up, ffn1_w_down, l)
        if l < N_A:
            xt = _mixer_a(xt, mix_norm, a_w_in, a_v_norm, a_w_spatial, a_b_spatial, mem_kv, a_w_out, l, b)
        else:
            xt = _mixer_b(xt, mix_norm, b_w_in.astype(BF16), k, v, mem_kv, b_w_out.astype(BF16), l, l - N_A)
        xt = _ffn(xt, ffn2_norm, ffn2_w_gate, ffn2_w_up, ffn2_w_down, l,
                  final_gain=final_norm if l == DEPTH - 1 else None)
    return xt.reshape(b, s, d)
```

```python
import functools

import jax
import jax.numpy as jnp
from jax import lax
from jax.experimental import pallas as pl
from jax.experimental.pallas import tpu as pltpu

D_MODEL = 1024
DEPTH = 4
N_A = DEPTH // 2
CHUNK = 64
N_MEM = 256
MEM_HEADS = 4
HEAD_DIM = 64
MEM_W = MEM_HEADS * HEAD_DIM
GM_W = D_MODEL - MEM_W
GM_GROUPS = 6
GM_GC = GM_W // GM_GROUPS
GM_CHUNK = 128
SB_W = D_MODEL - MEM_W
D_FF = 2816
EPS = 1e-6

LANES = 128
HEAD_PAIRS_SB = SB_W // LANES
HEAD_PAIRS_MEM = MEM_W // LANES
QK_SCALE = HEAD_DIM ** -0.5
LOG2E = 1.4426950408889634

TM = 1024
TM_FFN = 1024
FF_CHUNK = 256
TQ = 256
TK = 256
SB_DONE = 158.0
SB_NO_TILE = 1e30
VMEM_LIMIT = 60 * 1024 * 1024

BF16 = jnp.bfloat16
F32 = jnp.float32
_NT = (((1,), (1,)), ((), ()))


def _params(*sem):
    return pltpu.CompilerParams(dimension_semantics=sem, vmem_limit_bytes=VMEM_LIMIT)


def _rms(x, g):
    r = lax.rsqrt(jnp.mean(x * x, axis=-1, keepdims=True) + EPS)
    return x * r * g


def _dot(a, b):
    return jnp.dot(a, b, preferred_element_type=F32)


def _full(shape):
    return pl.BlockSpec(shape, lambda *_: (0,) * len(shape))


def _layer_block(shape, layer):
    return pl.BlockSpec((1,) + tuple(shape), lambda *_: (layer,) + (0,) * len(shape),
                        pipeline_mode=pl.Buffered(1))


def _mem_kv_block(layer, batch, seq):
    steps_per_batch = seq // TM
    return pl.BlockSpec((1, N_MEM, 2 * MEM_W), lambda i: (layer * batch + i // steps_per_batch, 0, 0))


def _ffn_kernel(x_ref, g_ref, wg_ref, wu_ref, wd_ref, *rest, final):
    if final:
        gf_ref, o_ref = rest
    else:
        o_ref, = rest
    h = _rms(x_ref[...], g_ref[0]).astype(BF16)
    for c in range(D_FF // FF_CHUNK):
        cols = slice(c * FF_CHUNK, (c + 1) * FF_CHUNK)
        gate = _dot(h, wg_ref[0, :, cols].astype(BF16))
        up = _dot(h, wu_ref[0, :, cols].astype(BF16))
        act = (gate * jax.nn.sigmoid(gate) * up).astype(BF16)
        part = _dot(act, wd_ref[0, cols, :].astype(BF16))
        if c == 0:
            o_ref[...] = part
        else:
            o_ref[...] += part
    y = x_ref[...] + 0.5 * o_ref[...]
    if final:
        y = _rms(y, gf_ref[...])
    o_ref[...] = y


def _ffn(x, g, wg, wu, wd, layer, final_gain=None):
    t, d = x.shape
    final = final_gain is not None
    row = pl.BlockSpec((TM_FFN, d), lambda i: (i, 0))
    in_specs = [row, _layer_block((1, d), layer), _layer_block((d, D_FF), layer),
                _layer_block((d, D_FF), layer), _layer_block((D_FF, d), layer)]
    args = [x, g.reshape(-1, 1, d), wg, wu, wd]
    if final:
        in_specs.append(_full((1, d)))
        args.append(final_gain.reshape(1, d))
    return pl.pallas_call(
        functools.partial(_ffn_kernel, final=final),
        grid=(t // TM_FFN,),
        in_specs=in_specs,
        out_specs=row,
        out_shape=jax.ShapeDtypeStruct((t, d), F32),
        compiler_params=_params("parallel"),
        name="ffn_final" if final else "ffn",
    )(*args)


def _mem_kv_kernel(mem_ref, g_ref, w_ref, o_ref):
    h = _rms(mem_ref[...], g_ref[...]).astype(BF16)
    o_ref[0] = _dot(h, w_ref[0].astype(BF16)).astype(BF16)


def _mem_kv(mem, g, w):
    n, d = mem.shape
    depth, _, width = w.shape
    return pl.pallas_call(
        _mem_kv_kernel,
        grid=(depth,),
        in_specs=[_full((n, d)), _full((1, d)), pl.BlockSpec((1, d, width), lambda l: (l, 0, 0))],
        out_specs=pl.BlockSpec((1, n, width), lambda l: (l, 0, 0)),
        out_shape=jax.ShapeDtypeStruct((depth, n, width), BF16),
        compiler_params=_params("parallel"),
        name="mem_kv",
    )(mem, g.reshape(1, d), w)


def _memory_attention(q_mem, kv_ref, y_ref, col0):
    rows = q_mem.shape[0]
    lane = lax.broadcasted_iota(jnp.int32, (rows, LANES), 1)
    low = lane < HEAD_DIM
    for p in range(HEAD_PAIRS_MEM):
        cols = slice(p * LANES, (p + 1) * LANES)
        q2 = q_mem[:, cols]
        k2 = kv_ref[:, cols]
        v2 = kv_ref[:, MEM_W + p * LANES:MEM_W + (p + 1) * LANES]
        outs = []
        for keep in (low, ~low):
            s = lax.dot_general(jnp.where(keep, q2, jnp.zeros_like(q2)), k2, _NT,
                                preferred_element_type=F32)
            e = jnp.exp(s - jnp.max(s, axis=-1, keepdims=True))
            inv = 1.0 / jnp.sum(e, axis=-1, keepdims=True)
            outs.append(_dot(e.astype(BF16), v2) * inv)
        y_ref[:, col0 + p * LANES:col0 + (p + 1) * LANES] = jnp.where(low, outs[0], outs[1]).astype(BF16)


def _mixer_a_kernel(x_ref, g_ref, win_ref, vg_ref, ws_ref, bs_ref, kv_ref, wout_ref, o_ref, y_ref):
    x = x_ref[...]
    h = _rms(x, g_ref[0]).astype(BF16)
    proj = _dot(h, win_ref[0].astype(BF16))
    u = jax.nn.gelu(proj[:, :GM_W])
    v = _rms(jax.nn.gelu(proj[:, GM_W:2 * GM_W]), vg_ref[0]).astype(BF16)
    pos_t = lax.broadcasted_iota(jnp.int32, (GM_CHUNK, GM_CHUNK), 0) // CHUNK
    pos_s = lax.broadcasted_iota(jnp.int32, (GM_CHUNK, GM_CHUNK), 1) // CHUNK
    chunk_causal = pos_s <= pos_t
    for gi in range(GM_GROUPS):
        cols = slice(gi * GM_GC, (gi + 1) * GM_GC)
        w = jnp.where(chunk_causal, ws_ref[0, gi], 0.0).astype(BF16)
        bias = bs_ref[0, gi]
        for n in range(TM // GM_CHUNK):
            rows = slice(n * GM_CHUNK, (n + 1) * GM_CHUNK)
            mixed = _dot(w, v[rows, cols]) + bias
            y_ref[rows, cols] = (u[rows, cols] * mixed).astype(BF16)
    q_mem = (proj[:, 2 * GM_W:] * QK_SCALE).astype(BF16)
    _memory_attention(q_mem, kv_ref.at[0], y_ref, GM_W)
    o_ref[...] = x + _dot(y_ref[...], wout_ref[0].astype(BF16))


def _mixer_a(x, g, w_in, v_gain, w_sp, b_sp, mem_kv, w_out, layer, batch):
    t, d = x.shape
    row = pl.BlockSpec((TM, d), lambda i: (i, 0))
    return pl.pallas_call(
        _mixer_a_kernel,
        grid=(t // TM,),
        in_specs=[
            row, _layer_block((1, d), layer), _layer_block(w_in.shape[1:], layer),
            _layer_block((1, GM_W), layer), _layer_block(w_sp.shape[1:], layer),
            _layer_block((GM_GROUPS, GM_CHUNK, 1), layer),
            _mem_kv_block(layer, batch, t // batch),
            _layer_block(w_out.shape[1:], layer),
        ],
        out_specs=row,
        out_shape=jax.ShapeDtypeStruct((t, d), F32),
        scratch_shapes=[pltpu.VMEM((TM, d), BF16)],
        compiler_params=_params("parallel"),
        name="mixer_a",
    )(x, g.reshape(-1, 1, d), w_in, v_gain.reshape(-1, 1, GM_W), w_sp,
      b_sp.reshape(-1, GM_GROUPS, GM_CHUNK, 1), mem_kv, w_out)


def _kv_proj_kernel(x_ref, g_ref, w_ref, k_ref, v_ref):
    h = _rms(x_ref[...], g_ref[...]).astype(BF16)
    kv = _dot(h, w_ref[...].astype(BF16))
    k_ref[...] = kv[:, :SB_W].astype(BF16)
    v_ref[...] = kv[:, SB_W:].astype(BF16)


def _kv_proj(x, g, w):
    t, d = x.shape
    out = pl.BlockSpec((TM, SB_W), lambda i: (i, 0))
    return pl.pallas_call(
        _kv_proj_kernel,
        grid=(t // TM,),
        in_specs=[pl.BlockSpec((TM, d), lambda i: (i, 0)), _full((1, d)), _full(w.shape)],
        out_specs=[out, out],
        out_shape=[jax.ShapeDtypeStruct((t, SB_W), BF16)] * 2,
        compiler_params=_params("parallel"),
        name="kv_proj",
    )(x, g.reshape(1, d), w)


def _b_in_kernel(x_ref, g_ref, w_ref, qsb_ref, qmem_ref):
    h = _rms(x_ref[...], g_ref[0]).astype(BF16)
    q = _dot(h, w_ref[0].astype(BF16))
    qsb_ref[...] = (q[:, :SB_W] * (QK_SCALE * LOG2E)).astype(BF16)
    qmem_ref[...] = (q[:, SB_W:] * QK_SCALE).astype(BF16)


def _b_in(x, g, w, layer, w_layer):
    t, d = x.shape
    return pl.pallas_call(
        _b_in_kernel,
        grid=(t // TM,),
        in_specs=[pl.BlockSpec((TM, d), lambda i: (i, 0)), _layer_block((1, d), layer),
                  _layer_block(w.shape[1:], w_layer)],
        out_specs=[pl.BlockSpec((TM, SB_W), lambda i: (i, 0)), pl.BlockSpec((TM, MEM_W), lambda i: (i, 0))],
        out_shape=[jax.ShapeDtypeStruct((t, SB_W), BF16), jax.ShapeDtypeStruct((t, MEM_W), BF16)],
        compiler_params=_params("parallel"),
        name="b_in",
    )(x, g.reshape(-1, 1, d), w)


def _b_out_kernel(x_ref, ytok_ref, qmem_ref, kv_ref, wout_ref, o_ref, y_ref):
    y_ref[:, :SB_W] = ytok_ref[...]
    _memory_attention(qmem_ref[...], kv_ref.at[0], y_ref, SB_W)
    o_ref[...] = x_ref[...] + _dot(y_ref[...], wout_ref[0].astype(BF16))


def _b_out(x, y_tok, q_mem, mem_kv, w_out, layer, w_layer, batch):
    t, d = x.shape
    row = pl.BlockSpec((TM, d), lambda i: (i, 0))
    return pl.pallas_call(
        _b_out_kernel,
        grid=(t // TM,),
        in_specs=[
            row,
            pl.BlockSpec((TM, SB_W), lambda i: (i, 0)),
            pl.BlockSpec((TM, MEM_W), lambda i: (i, 0)),
            _mem_kv_block(layer, batch, t // batch),
            _layer_block(w_out.shape[1:], w_layer),
        ],
        out_specs=row,
        out_shape=jax.ShapeDtypeStruct((t, d), F32),
        scratch_shapes=[pltpu.VMEM((TM, d), BF16)],
        compiler_params=_params("parallel"),
        name="b_out",
    )(x, y_tok, q_mem, mem_kv, w_out)


def _softplus2(z):
    return jnp.maximum(z, 0.0) + jnp.log2(1.0 + jnp.exp2(-jnp.abs(z)))


def _sb_attn_kernel(q_ref, k_ref, v_ref, o_ref):
    i = pl.program_id(1)
    lane = lax.broadcasted_iota(jnp.int32, (TQ, LANES), 1)
    low = lane < HEAD_DIM
    kj = lax.broadcasted_iota(jnp.int32, (TK, TK), 0)
    ks = lax.broadcasted_iota(jnp.int32, (TK, TK), 1)
    suffix_ones = (kj >= ks).astype(BF16)

    qhs = []
    for p in range(HEAD_PAIRS_SB):
        q2 = q_ref[0, :, p * LANES:(p + 1) * LANES]
        qhs += [jnp.where(keep, q2, jnp.zeros_like(q2)) for keep in (low, ~low)]
    n_chains = 2 * HEAD_PAIRS_SB

    half = TQ // 2
    diag_pieces = []
    for r0, r1, k1 in ((0, half, half), (half, TQ, TK)):
        row = lax.broadcasted_iota(jnp.int32, (r1 - r0, k1), 0) + r0
        col = lax.broadcasted_iota(jnp.int32, (r1 - r0, k1), 1)
        diag_pieces.append((r0, r1, 0, k1, col < row))
    prev_pieces = [(0, half, 0, TK, None), (half, TQ, half, TK, None)]
    row = lax.broadcasted_iota(jnp.int32, (TQ, TK), 0)
    col = lax.broadcasted_iota(jnp.int32, (TQ, TK), 1)
    left_out = jnp.logical_and(row >= half, col < half)

    def scores(j, pieces):
        start = pl.multiple_of(j * TK, TK)
        kts = [k_ref[0, pl.ds(start, TK), p * LANES:(p + 1) * LANES] for p in range(HEAD_PAIRS_SB)]
        vts = [v_ref[0, pl.ds(start, TK), p * LANES:(p + 1) * LANES] for p in range(HEAD_PAIRS_SB)]
        out = []
        for r0, r1, k0, k1, mask in pieces:
            zs = [lax.dot_general(qhs[ch][r0:r1], kts[ch // 2][k0:k1], _NT, preferred_element_type=F32)
                  for ch in range(n_chains)]
            sps = [_softplus2(z) for z in zs]
            if mask is not None:
                sps = [jnp.where(mask, sp, 0.0) for sp in sps]
            sufs = [_dot(sp.astype(BF16), suffix_ones[:k1 - k0, :k1 - k0]) for sp in sps]
            out.append((zs, sufs, [vts[ch // 2][k0:k1] for ch in range(n_chains)]))
        return out

    def accumulate(tile, pieces, cs, accs):
        new_cs, new_accs = [[] for _ in range(n_chains)], [[] for _ in range(n_chains)]
        covered = 0
        for (r0, r1, _, _, mask), (zs, sufs, vts) in zip(pieces, tile):
            ws = [jnp.exp2(z - suf - c[r0:r1]) for z, suf, c in zip(zs, sufs, cs)]
            if mask is not None:
                ws = [jnp.where(mask, w, 0.0) for w in ws]
            for ch in range(n_chains):
                if r0 > covered:
                    new_accs[ch].append(jnp.zeros((r0 - covered, LANES), F32))
                    new_cs[ch].append(jnp.zeros((r0 - covered, 1), F32))
                new_accs[ch].append(_dot(ws[ch].astype(BF16), vts[ch]))
                new_cs[ch].append(sufs[ch][:, :1])
            covered = r1
        assert covered == TQ
        accs = tuple(acc + jnp.concatenate(parts, axis=0) for acc, parts in zip(accs, new_accs))
        cs = tuple(c + jnp.concatenate(parts, axis=0) for c, parts in zip(cs, new_cs))
        return cs, accs

    def c_min_of(cs):
        return functools.reduce(jnp.minimum, [jnp.min(c) for c in cs])

    j_prev = jnp.maximum(i - 1, 0)
    diag = scores(i, diag_pieces)
    prev = scores(j_prev, prev_pieces)
    cs, accs = accumulate(diag, diag_pieces, (jnp.zeros((TQ, 1), F32),) * n_chains,
                          (jnp.zeros((TQ, LANES), F32),) * n_chains)
    cs = tuple(jnp.where(i > 0, c, SB_NO_TILE) for c in cs)
    cs, accs = accumulate(prev, prev_pieces, cs, accs)

    def keep_going(carry):
        n, c_min, _, _ = carry
        return jnp.logical_and(n < i, c_min < SB_DONE)

    def step(carry):
        n, _, cs, accs = carry
        pieces = [(0, TQ, 0, TK, jnp.logical_or(n > 0, left_out))]
        cs, accs = accumulate(scores(i - 1 - n, pieces), pieces, cs, accs)
        return n + 1, c_min_of(cs), cs, accs

    _, _, _, accs = lax.while_loop(keep_going, step, (jnp.int32(0), c_min_of(cs), cs, accs))
    for p in range(HEAD_PAIRS_SB):
        o_ref[0, :, p * LANES:(p + 1) * LANES] = jnp.where(low, accs[2 * p], accs[2 * p + 1]).astype(BF16)


def _sb_attn(q, k, v):
    b, s, _ = q.shape
    qspec = pl.BlockSpec((1, TQ, SB_W), lambda bi, i: (bi, i, 0))
    kvspec = pl.BlockSpec((1, s, SB_W), lambda bi, i: (bi, 0, 0), pipeline_mode=pl.Buffered(1))
    return pl.pallas_call(
        _sb_attn_kernel,
        grid=(b, s // TQ),
        in_specs=[qspec, kvspec, kvspec],
        out_specs=qspec,
        out_shape=jax.ShapeDtypeStruct((b, s, SB_W), BF16),
        compiler_params=_params("parallel", "arbitrary"),
        name="sb_attn",
    )(q, k, v)


def kernel(x, mem, ffn1_norm, ffn1_w_gate, ffn1_w_up, ffn1_w_down, mix_norm, ffn2_norm, ffn2_w_gate, ffn2_w_up, ffn2_w_down, mem_norm, w_mem_kv, a_w_in, a_v_norm, a_w_spatial, a_b_spatial, a_w_out, kv_norm, w_kv, b_w_in, b_w_out, final_norm):
    b, s, d = x.shape
    assert d == D_MODEL and s % TM == 0 and s % TM_FFN == 0 and s % TQ == 0 and TQ == TK
    t = b * s
    xt = x.reshape(t, d)
    mem_kv = _mem_kv(mem.reshape(b * N_MEM, d), mem_norm, w_mem_kv)
    mem_kv = mem_kv.reshape(DEPTH * b, N_MEM, 2 * MEM_W)
    k = v = None
    for l in range(DEPTH):
        if l == N_A:
            k, v = _kv_proj(xt, kv_norm, w_kv)
            k = k.reshape(b, s, SB_W)
            v = v.reshape(b, s, SB_W)
        xt = _ffn(xt, ffn1_norm, ffn1_w_gate, ffn1_w_up, ffn1_w_down, l)
        if l < N_A:
            xt = _mixer_a(xt, mix_norm, a_w_in, a_v_norm, a_w_spatial, a_b_spatial, mem_kv, a_w_out, l, b)
        else:
            q_sb, q_mem = _b_in(xt, mix_norm, b_w_in, l, l - N_A)
            y_tok = _sb_attn(q_sb.reshape(b, s, SB_W), k, v).reshape(t, SB_W)
            xt = _b_out(xt, y_tok, q_mem, mem_kv, b_w_out, l, l - N_A, b)
        xt = _ffn(xt, ffn2_norm, ffn2_w_gate, ffn2_w_up, ffn2_w_down, l,
                  final_gain=final_norm if l == DEPTH - 1 else None)
    return xt.reshape(b, s, d)
```

```python
import functools

import jax
import jax.numpy as jnp
from jax import lax
from jax.experimental import pallas as pl
from jax.experimental.pallas import tpu as pltpu

D_MODEL = 1024
DEPTH = 4
N_A = DEPTH // 2
CHUNK = 64
N_MEM = 256
MEM_HEADS = 4
HEAD_DIM = 64
MEM_W = MEM_HEADS * HEAD_DIM
GM_W = D_MODEL - MEM_W
GM_GROUPS = 6
GM_GC = GM_W // GM_GROUPS
GM_CHUNK = 128
SB_W = D_MODEL - MEM_W
D_FF = 2816
EPS = 1e-6

LANES = 128
HEAD_PAIRS_SB = SB_W // LANES
HEAD_PAIRS_MEM = MEM_W // LANES
QK_SCALE = HEAD_DIM ** -0.5
LOG2E = 1.4426950408889634

TM = 1024
TM_FFN = 1024
FF_CHUNK = 256
TQ = 256
TK = 256
SB_DONE = 158.0
SB_NO_TILE = 1e30
VMEM_LIMIT = 60 * 1024 * 1024

BF16 = jnp.bfloat16
F32 = jnp.float32
_NT = (((1,), (1,)), ((), ()))


def _params(*sem):
    return pltpu.CompilerParams(dimension_semantics=sem, vmem_limit_bytes=VMEM_LIMIT)


def _rms(x, g):
    r = lax.rsqrt(jnp.mean(x * x, axis=-1, keepdims=True) + EPS)
    return x * r * g


def _dot(a, b):
    return jnp.dot(a, b, preferred_element_type=F32)


def _full(shape):
    return pl.BlockSpec(shape, lambda *_: (0,) * len(shape))


def _layer_block(shape, layer):
    return pl.BlockSpec((1,) + tuple(shape), lambda *_: (layer,) + (0,) * len(shape),
                        pipeline_mode=pl.Buffered(1))


def _mem_kv_block(layer, batch, seq):
    steps_per_batch = seq // TM
    return pl.BlockSpec((1, N_MEM, 2 * MEM_W), lambda i: (layer * batch + i // steps_per_batch, 0, 0))


def _ffn_kernel(x_ref, g_ref, wg_ref, wu_ref, wd_ref, *rest, final):
    if final:
        gf_ref, o_ref = rest
    else:
        o_ref, = rest
    h = _rms(x_ref[...], g_ref[0]).astype(BF16)
    for c in range(D_FF // FF_CHUNK):
        cols = slice(c * FF_CHUNK, (c + 1) * FF_CHUNK)
        gate = _dot(h, wg_ref[0, :, cols].astype(BF16))
        up = _dot(h, wu_ref[0, :, cols].astype(BF16))
        act = (gate * jax.nn.sigmoid(gate) * up).astype(BF16)
        part = _dot(act, wd_ref[0, cols, :].astype(BF16))
        if c == 0:
            o_ref[...] = part
        else:
            o_ref[...] += part
    y = x_ref[...] + 0.5 * o_ref[...]
    if final:
        y = _rms(y, gf_ref[...])
    o_ref[...] = y


def _ffn(x, g, wg, wu, wd, layer, final_gain=None):
    t, d = x.shape
    final = final_gain is not None
    row = pl.BlockSpec((TM_FFN, d), lambda i: (i, 0))
    in_specs = [row, _layer_block((1, d), layer), _layer_block((d, D_FF), layer),
                _layer_block((d, D_FF), layer), _layer_block((D_FF, d), layer)]
    args = [x, g.reshape(-1, 1, d), wg, wu, wd]
    if final:
        in_specs.append(_full((1, d)))
        args.append(final_gain.reshape(1, d))
    return pl.pallas_call(
        functools.partial(_ffn_kernel, final=final),
        grid=(t // TM_FFN,),
        in_specs=in_specs,
        out_specs=row,
        out_shape=jax.ShapeDtypeStruct((t, d), F32),
        compiler_params=_params("parallel"),
        name="ffn_final" if final else "ffn",
    )(*args)


def _mem_kv_kernel(mem_ref, g_ref, w_ref, o_ref):
    h = _rms(mem_ref[...], g_ref[...]).astype(BF16)
    o_ref[0] = _dot(h, w_ref[0].astype(BF16)).astype(BF16)


def _mem_kv(mem, g, w):
    n, d = mem.shape
    depth, _, width = w.shape
    return pl.pallas_call(
        _mem_kv_kernel,
        grid=(depth,),
        in_specs=[_full((n, d)), _full((1, d)), pl.BlockSpec((1, d, width), lambda l: (l, 0, 0))],
        out_specs=pl.BlockSpec((1, n, width), lambda l: (l, 0, 0)),
        out_shape=jax.ShapeDtypeStruct((depth, n, width), BF16),
        compiler_params=_params("parallel"),
        name="mem_kv",
    )(mem, g.reshape(1, d), w)


def _memory_attention(q_mem, kv_ref, y_ref, col0):
    rows = q_mem.shape[0]
    lane = lax.broadcasted_iota(jnp.int32, (rows, LANES), 1)
    low = lane < HEAD_DIM
    for p in range(HEAD_PAIRS_MEM):
        cols = slice(p * LANES, (p + 1) * LANES)
        q2 = q_mem[:, cols]
        k2 = kv_ref[:, cols]
        v2 = kv_ref[:, MEM_W + p * LANES:MEM_W + (p + 1) * LANES]
        outs = []
        for keep in (low, ~low):
            s = lax.dot_general(jnp.where(keep, q2, jnp.zeros_like(q2)), k2, _NT,
                                preferred_element_type=F32)
            e = jnp.exp(s - jnp.max(s, axis=-1, keepdims=True))
            inv = 1.0 / jnp.sum(e, axis=-1, keepdims=True)
            outs.append(_dot(e.astype(BF16), v2) * inv)
        y_ref[:, col0 + p * LANES:col0 + (p + 1) * LANES] = jnp.where(low, outs[0], outs[1]).astype(BF16)


def _mixer_a_kernel(x_ref, g_ref, win_ref, vg_ref, ws_ref, bs_ref, kv_ref, wout_ref, o_ref, y_ref):
    x = x_ref[...]
    h = _rms(x, g_ref[0]).astype(BF16)
    proj = _dot(h, win_ref[0].astype(BF16))
    u = jax.nn.gelu(proj[:, :GM_W])
    v = _rms(jax.nn.gelu(proj[:, GM_W:2 * GM_W]), vg_ref[0]).astype(BF16)
    pos_t = lax.broadcasted_iota(jnp.int32, (GM_CHUNK, GM_CHUNK), 0) // CHUNK
    pos_s = lax.broadcasted_iota(jnp.int32, (GM_CHUNK, GM_CHUNK), 1) // CHUNK
    chunk_causal = pos_s <= pos_t
    for gi in range(GM_GROUPS):
        cols = slice(gi * GM_GC, (gi + 1) * GM_GC)
        w = jnp.where(chunk_causal, ws_ref[0, gi], 0.0).astype(BF16)
        bias = bs_ref[0, gi]
        for n in range(TM // GM_CHUNK):
            rows = slice(n * GM_CHUNK, (n + 1) * GM_CHUNK)
            mixed = _dot(w, v[rows, cols]) + bias
            y_ref[rows, cols] = (u[rows, cols] * mixed).astype(BF16)
    q_mem = (proj[:, 2 * GM_W:] * QK_SCALE).astype(BF16)
    _memory_attention(q_mem, kv_ref.at[0], y_ref, GM_W)
    o_ref[...] = x + _dot(y_ref[...], wout_ref[0].astype(BF16))


def _mixer_a(x, g, w_in, v_gain, w_sp, b_sp, mem_kv, w_out, layer, batch):
    t, d = x.shape
    row = pl.BlockSpec((TM, d), lambda i: (i, 0))
    return pl.pallas_call(
        _mixer_a_kernel,
        grid=(t // TM,),
        in_specs=[
            row, _layer_block((1, d), layer), _layer_block(w_in.shape[1:], layer),
            _layer_block((1, GM_W), layer), _layer_block(w_sp.shape[1:], layer),
            _layer_block((GM_GROUPS, GM_CHUNK, 1), layer),
            _mem_kv_block(layer, batch, t // batch),
            _layer_block(w_out.shape[1:], layer),
        ],
        out_specs=row,
        out_shape=jax.ShapeDtypeStruct((t, d), F32),
        scratch_shapes=[pltpu.VMEM((TM, d), BF16)],
        compiler_params=_params("parallel"),
        name="mixer_a",
    )(x, g.reshape(-1, 1, d), w_in, v_gain.reshape(-1, 1, GM_W), w_sp,
      b_sp.reshape(-1, GM_GROUPS, GM_CHUNK, 1), mem_kv, w_out)


def _kv_proj_kernel(x_ref, g_ref, w_ref, k_ref, v_ref):
    h = _rms(x_ref[...], g_ref[...]).astype(BF16)
    kv = _dot(h, w_ref[...].astype(BF16))
    k_ref[...] = kv[:, :SB_W].astype(BF16)
    v_ref[...] = kv[:, SB_W:].astype(BF16)


def _kv_proj(x, g, w):
    t, d = x.shape
    out = pl.BlockSpec((TM, SB_W), lambda i: (i, 0))
    return pl.pallas_call(
        _kv_proj_kernel,
        grid=(t // TM,),
        in_specs=[pl.BlockSpec((TM, d), lambda i: (i, 0)), _full((1, d)), _full(w.shape)],
        out_specs=[out, out],
        out_shape=[jax.ShapeDtypeStruct((t, SB_W), BF16)] * 2,
        compiler_params=_params("parallel"),
        name="kv_proj",
    )(x, g.reshape(1, d), w)


def _b_in_kernel(x_ref, g_ref, w_ref, qsb_ref, qmem_ref):
    h = _rms(x_ref[...], g_ref[0]).astype(BF16)
    q = _dot(h, w_ref[0].astype(BF16))
    qsb_ref[...] = (q[:, :SB_W] * (QK_SCALE * LOG2E)).astype(BF16)
    qmem_ref[...] = (q[:, SB_W:] * QK_SCALE).astype(BF16)


def _b_in(x, g, w, layer, w_layer):
    t, d = x.shape
    return pl.pallas_call(
        _b_in_kernel,
        grid=(t // TM,),
        in_specs=[pl.BlockSpec((TM, d), lambda i: (i, 0)), _layer_block((1, d), layer),
                  _layer_block(w.shape[1:], w_layer)],
        out_specs=[pl.BlockSpec((TM, SB_W), lambda i: (i, 0)), pl.BlockSpec((TM, MEM_W), lambda i: (i, 0))],
        out_shape=[jax.ShapeDtypeStruct((t, SB_W), BF16), jax.ShapeDtypeStruct((t, MEM_W), BF16)],
        compiler_params=_params("parallel"),
        name="b_in",
    )(x, g.reshape(-1, 1, d), w)


def _b_out_kernel(x_ref, ytok_ref, qmem_ref, kv_ref, wout_ref, o_ref, y_ref):
    y_ref[:, :SB_W] = ytok_ref[...]
    _memory_attention(qmem_ref[...], kv_ref.at[0], y_ref, SB_W)
    o_ref[...] = x_ref[...] + _dot(y_ref[...], wout_ref[0].astype(BF16))


def _b_out(x, y_tok, q_mem, mem_kv, w_out, layer, w_layer, batch):
    t, d = x.shape
    row = pl.BlockSpec((TM, d), lambda i: (i, 0))
    return pl.pallas_call(
        _b_out_kernel,
        grid=(t // TM,),
        in_specs=[
            row,
            pl.BlockSpec((TM, SB_W), lambda i: (i, 0)),
            pl.BlockSpec((TM, MEM_W), lambda i: (i, 0)),
            _mem_kv_block(layer, batch, t // batch),
            _layer_block(w_out.shape[1:], w_layer),
        ],
        out_specs=row,
        out_shape=jax.ShapeDtypeStruct((t, d), F32),
        scratch_shapes=[pltpu.VMEM((TM, d), BF16)],
        compiler_params=_params("parallel"),
        name="b_out",
    )(x, y_tok, q_mem, mem_kv, w_out)


def _softplus2(z):
    return jnp.maximum(z, 0.0) + jnp.log2(1.0 + jnp.exp2(-jnp.abs(z)))


def _sb_attn_kernel(q_ref, k_ref, v_ref, o_ref):
    i = pl.program_id(1)
    lane = lax.broadcasted_iota(jnp.int32, (TQ, LANES), 1)
    low = lane < HEAD_DIM
    kj = lax.broadcasted_iota(jnp.int32, (TK, TK), 0)
    ks = lax.broadcasted_iota(jnp.int32, (TK, TK), 1)
    suffix_ones = (kj >= ks).astype(BF16)

    qhs = []
    for p in range(HEAD_PAIRS_SB):
        q2 = q_ref[0, :, p * LANES:(p + 1) * LANES]
        qhs += [jnp.where(keep, q2, jnp.zeros_like(q2)) for keep in (low, ~low)]
    n_chains = 2 * HEAD_PAIRS_SB

    half = TQ // 2
    diag_pieces = []
    for r0, r1, k1 in ((0, half, half), (half, TQ, TK)):
        row = lax.broadcasted_iota(jnp.int32, (r1 - r0, k1), 0) + r0
        col = lax.broadcasted_iota(jnp.int32, (r1 - r0, k1), 1)
        diag_pieces.append((r0, r1, 0, k1, col < row))
    quarter = TQ // 4
    prev_pieces = [(0, quarter, 0, TK, None), (quarter, TQ - quarter, half, TK, None)]
    row = lax.broadcasted_iota(jnp.int32, (TQ, TK), 0)
    col = lax.broadcasted_iota(jnp.int32, (TQ, TK), 1)
    left_out = jnp.logical_or(jnp.logical_and(row >= quarter, col < half), row >= TQ - quarter)

    def scores(j, pieces):
        start = pl.multiple_of(j * TK, TK)
        kts = [k_ref[0, pl.ds(start, TK), p * LANES:(p + 1) * LANES] for p in range(HEAD_PAIRS_SB)]
        vts = [v_ref[0, pl.ds(start, TK), p * LANES:(p + 1) * LANES] for p in range(HEAD_PAIRS_SB)]
        out = []
        for r0, r1, k0, k1, mask in pieces:
            zs = [lax.dot_general(qhs[ch][r0:r1], kts[ch // 2][k0:k1], _NT, preferred_element_type=F32)
                  for ch in range(n_chains)]
            sps = [_softplus2(z) for z in zs]
            if mask is not None:
                sps = [jnp.where(mask, sp, 0.0) for sp in sps]
            sufs = [_dot(sp.astype(BF16), suffix_ones[:k1 - k0, :k1 - k0]) for sp in sps]
            out.append((zs, sufs, [vts[ch // 2][k0:k1] for ch in range(n_chains)]))
        return out

    def accumulate(tile, pieces, cs, accs):
        new_cs, new_accs = [[] for _ in range(n_chains)], [[] for _ in range(n_chains)]
        covered = 0
        for (r0, r1, _, _, mask), (zs, sufs, vts) in zip(pieces, tile):
            ws = [jnp.exp2(z - suf - c[r0:r1]) for z, suf, c in zip(zs, sufs, cs)]
            if mask is not None:
                ws = [jnp.where(mask, w, 0.0) for w in ws]
            for ch in range(n_chains):
                if r0 > covered:
                    new_accs[ch].append(accs[ch][covered:r0])
                    new_cs[ch].append(cs[ch][covered:r0])
                new_accs[ch].append(accs[ch][r0:r1] + _dot(ws[ch].astype(BF16), vts[ch]))
                new_cs[ch].append(cs[ch][r0:r1] + sufs[ch][:, :1])
            covered = r1
        if covered < TQ:
            for ch in range(n_chains):
                new_accs[ch].append(accs[ch][covered:])
                new_cs[ch].append(cs[ch][covered:])
        accs = tuple(jnp.concatenate(parts, axis=0) for parts in new_accs)
        cs = tuple(jnp.concatenate(parts, axis=0) for parts in new_cs)
        return cs, accs

    def c_min_of(cs):
        return functools.reduce(jnp.minimum, [jnp.min(c) for c in cs])

    j_prev = jnp.maximum(i - 1, 0)
    diag = scores(i, diag_pieces)
    prev = scores(j_prev, prev_pieces)
    cs, accs = accumulate(diag, diag_pieces, (jnp.zeros((TQ, 1), F32),) * n_chains,
                          (jnp.zeros((TQ, LANES), F32),) * n_chains)
    cs = tuple(jnp.where(i > 0, c, SB_NO_TILE) for c in cs)
    cs, accs = accumulate(prev, prev_pieces, cs, accs)

    def keep_going(carry):
        n, c_min, _, _ = carry
        return jnp.logical_and(n < i, c_min < SB_DONE)

    def step(carry):
        n, _, cs, accs = carry
        pieces = [(0, TQ, 0, TK, jnp.logical_or(n > 0, left_out))]
        cs, accs = accumulate(scores(i - 1 - n, pieces), pieces, cs, accs)
        return n + 1, c_min_of(cs), cs, accs

    _, _, _, accs = lax.while_loop(keep_going, step, (jnp.int32(0), c_min_of(cs), cs, accs))
    for p in range(HEAD_PAIRS_SB):
        o_ref[0, :, p * LANES:(p + 1) * LANES] = jnp.where(low, accs[2 * p], accs[2 * p + 1]).astype(BF16)


def _sb_attn(q, k, v):
    b, s, _ = q.shape
    qspec = pl.BlockSpec((1, TQ, SB_W), lambda bi, i: (bi, i, 0))
    kvspec = pl.BlockSpec((1, s, SB_W), lambda bi, i: (bi, 0, 0), pipeline_mode=pl.Buffered(1))
    return pl.pallas_call(
        _sb_attn_kernel,
        grid=(b, s // TQ),
        in_specs=[qspec, kvspec, kvspec],
        out_specs=qspec,
        out_shape=jax.ShapeDtypeStruct((b, s, SB_W), BF16),
        compiler_params=_params("parallel", "arbitrary"),
        name="sb_attn",
    )(q, k, v)


def kernel(x, mem, ffn1_norm, ffn1_w_gate, ffn1_w_up, ffn1_w_down, mix_norm, ffn2_norm, ffn2_w_gate, ffn2_w_up, ffn2_w_down, mem_norm, w_mem_kv, a_w_in, a_v_norm, a_w_spatial, a_b_spatial, a_w_out, kv_norm, w_kv, b_w_in, b_w_out, final_norm):
    b, s, d = x.shape
    assert d == D_MODEL and s % TM == 0 and s % TM_FFN == 0 and s % TQ == 0 and TQ == TK
    t = b * s
    xt = x.reshape(t, d)
    mem_kv = _mem_kv(mem.reshape(b * N_MEM, d), mem_norm, w_mem_kv)
    mem_kv = mem_kv.reshape(DEPTH * b, N_MEM, 2 * MEM_W)
    k = v = None
    for l in range(DEPTH):
        if l == N_A:
            k, v = _kv_proj(xt, kv_norm, w_kv)
            k = k.reshape(b, s, SB_W)
            v = v.reshape(b, s, SB_W)
        xt = _ffn(xt, ffn1_norm, ffn1_w_gate, ffn1_w_up, ffn1_w_down, l)
        if l < N_A:
            xt = _mixer_a(xt, mix_norm, a_w_in, a_v_norm, a_w_spatial, a_b_spatial, mem_kv, a_w_out, l, b)
        else:
            q_sb, q_mem = _b_in(xt, mix_norm, b_w_in, l, l - N_A)
            y_tok = _sb_attn(q_sb.reshape(b, s, SB_W), k, v).reshape(t, SB_W)
            xt = _b_out(xt, y_tok, q_mem, mem_kv, b_w_out, l, l - N_A, b)
        xt = _ffn(xt, ffn2_norm, ffn2_w_gate, ffn2_w_up, ffn2_w_down, l,
                  final_gain=final_norm if l == DEPTH - 1 else None)
    return xt.reshape(b, s, d)
```

```python
import functools

import jax
import jax.numpy as jnp
from jax import lax
from jax.experimental import pallas as pl
from jax.experimental.pallas import tpu as pltpu

D_MODEL = 1024
DEPTH = 4
N_A = DEPTH // 2
CHUNK = 64
N_MEM = 256
MEM_HEADS = 4
HEAD_DIM = 64
MEM_W = MEM_HEADS * HEAD_DIM
GM_W = D_MODEL - MEM_W
GM_GROUPS = 6
GM_GC = GM_W // GM_GROUPS
GM_CHUNK = 128
SB_W = D_MODEL - MEM_W
D_FF = 2816
EPS = 1e-6

LANES = 128
HEAD_PAIRS_SB = SB_W // LANES
HEAD_PAIRS_MEM = MEM_W // LANES
QK_SCALE = HEAD_DIM ** -0.5
LOG2E = 1.4426950408889634

TM = 1024
TM_FFN = 1024
FF_CHUNK = 256
TQ = 256
TK = 256
SB_DONE = 158.0
SB_NO_TILE = 1e30
VMEM_LIMIT = 60 * 1024 * 1024

BF16 = jnp.bfloat16
F32 = jnp.float32
_NT = (((1,), (1,)), ((), ()))


def _params(*sem):
    return pltpu.CompilerParams(dimension_semantics=sem, vmem_limit_bytes=VMEM_LIMIT)


def _rms(x, g):
    r = lax.rsqrt(jnp.mean(x * x, axis=-1, keepdims=True) + EPS)
    return x * r * g


def _dot(a, b):
    return jnp.dot(a, b, preferred_element_type=F32)


def _full(shape):
    return pl.BlockSpec(shape, lambda *_: (0,) * len(shape))


def _layer_block(shape, layer):
    return pl.BlockSpec((1,) + tuple(shape), lambda *_: (layer,) + (0,) * len(shape),
                        pipeline_mode=pl.Buffered(1))


def _mem_kv_block(layer, batch, seq):
    steps_per_batch = seq // TM
    return pl.BlockSpec((1, N_MEM, 2 * MEM_W), lambda i: (layer * batch + i // steps_per_batch, 0, 0))


def _ffn_kernel(x_ref, g_ref, wg_ref, wu_ref, wd_ref, *rest, final):
    if final:
        gf_ref, o_ref = rest
    else:
        o_ref, = rest
    h = _rms(x_ref[...], g_ref[0]).astype(BF16)
    for c in range(D_FF // FF_CHUNK):
        cols = slice(c * FF_CHUNK, (c + 1) * FF_CHUNK)
        gate = _dot(h, wg_ref[0, :, cols].astype(BF16))
        up = _dot(h, wu_ref[0, :, cols].astype(BF16))
        act = (gate * jax.nn.sigmoid(gate) * up).astype(BF16)
        part = _dot(act, wd_ref[0, cols, :].astype(BF16))
        if c == 0:
            o_ref[...] = part
        else:
            o_ref[...] += part
    y = x_ref[...] + 0.5 * o_ref[...]
    if final:
        y = _rms(y, gf_ref[...])
    o_ref[...] = y


def _ffn(x, g, wg, wu, wd, layer, final_gain=None):
    t, d = x.shape
    final = final_gain is not None
    row = pl.BlockSpec((TM_FFN, d), lambda i: (i, 0))
    in_specs = [row, _layer_block((1, d), layer), _layer_block((d, D_FF), layer),
                _layer_block((d, D_FF), layer), _layer_block((D_FF, d), layer)]
    args = [x, g.reshape(-1, 1, d), wg, wu, wd]
    if final:
        in_specs.append(_full((1, d)))
        args.append(final_gain.reshape(1, d))
    return pl.pallas_call(
        functools.partial(_ffn_kernel, final=final),
        grid=(t // TM_FFN,),
        in_specs=in_specs,
        out_specs=row,
        out_shape=jax.ShapeDtypeStruct((t, d), F32),
        compiler_params=_params("parallel"),
        name="ffn_final" if final else "ffn",
    )(*args)


def _mem_kv_kernel(mem_ref, g_ref, w_ref, o_ref):
    h = _rms(mem_ref[...], g_ref[...]).astype(BF16)
    o_ref[0] = _dot(h, w_ref[0].astype(BF16)).astype(BF16)


def _mem_kv(mem, g, w):
    n, d = mem.shape
    depth, _, width = w.shape
    return pl.pallas_call(
        _mem_kv_kernel,
        grid=(depth,),
        in_specs=[_full((n, d)), _full((1, d)), pl.BlockSpec((1, d, width), lambda l: (l, 0, 0))],
        out_specs=pl.BlockSpec((1, n, width), lambda l: (l, 0, 0)),
        out_shape=jax.ShapeDtypeStruct((depth, n, width), BF16),
        compiler_params=_params("parallel"),
        name="mem_kv",
    )(mem, g.reshape(1, d), w)


def _memory_attention(q_mem, kv_ref, y_ref, col0):
    rows = q_mem.shape[0]
    lane = lax.broadcasted_iota(jnp.int32, (rows, LANES), 1)
    low = lane < HEAD_DIM
    for p in range(HEAD_PAIRS_MEM):
        cols = slice(p * LANES, (p + 1) * LANES)
        q2 = q_mem[:, cols]
        k2 = kv_ref[:, cols]
        v2 = kv_ref[:, MEM_W + p * LANES:MEM_W + (p + 1) * LANES]
        outs = []
        for keep in (low, ~low):
            s = lax.dot_general(jnp.where(keep, q2, jnp.zeros_like(q2)), k2, _NT,
                                preferred_element_type=F32)
            e = jnp.exp(s - jnp.max(s, axis=-1, keepdims=True))
            inv = 1.0 / jnp.sum(e, axis=-1, keepdims=True)
            outs.append(_dot(e.astype(BF16), v2) * inv)
        y_ref[:, col0 + p * LANES:col0 + (p + 1) * LANES] = jnp.where(low, outs[0], outs[1]).astype(BF16)


def _mixer_a_kernel(x_ref, g_ref, win_ref, vg_ref, ws_ref, bs_ref, kv_ref, wout_ref, o_ref, y_ref):
    x = x_ref[...]
    h = _rms(x, g_ref[0]).astype(BF16)
    proj = _dot(h, win_ref[0].astype(BF16))
    u = jax.nn.gelu(proj[:, :GM_W])
    v = _rms(jax.nn.gelu(proj[:, GM_W:2 * GM_W]), vg_ref[0]).astype(BF16)
    pos_t = lax.broadcasted_iota(jnp.int32, (GM_CHUNK, GM_CHUNK), 0) // CHUNK
    pos_s = lax.broadcasted_iota(jnp.int32, (GM_CHUNK, GM_CHUNK), 1) // CHUNK
    chunk_causal = pos_s <= pos_t
    for gi in range(GM_GROUPS):
        cols = slice(gi * GM_GC, (gi + 1) * GM_GC)
        w = jnp.where(chunk_causal, ws_ref[0, gi], 0.0).astype(BF16)
        bias = bs_ref[0, gi]
        for n in range(TM // GM_CHUNK):
            rows = slice(n * GM_CHUNK, (n + 1) * GM_CHUNK)
            mixed = _dot(w, v[rows, cols]) + bias
            y_ref[rows, cols] = (u[rows, cols] * mixed).astype(BF16)
    q_mem = (proj[:, 2 * GM_W:] * QK_SCALE).astype(BF16)
    _memory_attention(q_mem, kv_ref.at[0], y_ref, GM_W)
    o_ref[...] = x + _dot(y_ref[...], wout_ref[0].astype(BF16))


def _mixer_a(x, g, w_in, v_gain, w_sp, b_sp, mem_kv, w_out, layer, batch):
    t, d = x.shape
    row = pl.BlockSpec((TM, d), lambda i: (i, 0))
    return pl.pallas_call(
        _mixer_a_kernel,
        grid=(t // TM,),
        in_specs=[
            row, _layer_block((1, d), layer), _layer_block(w_in.shape[1:], layer),
            _layer_block((1, GM_W), layer), _layer_block(w_sp.shape[1:], layer),
            _layer_block((GM_GROUPS, GM_CHUNK, 1), layer),
            _mem_kv_block(layer, batch, t // batch),
            _layer_block(w_out.shape[1:], layer),
        ],
        out_specs=row,
        out_shape=jax.ShapeDtypeStruct((t, d), F32),
        scratch_shapes=[pltpu.VMEM((TM, d), BF16)],
        compiler_params=_params("parallel"),
        name="mixer_a",
    )(x, g.reshape(-1, 1, d), w_in, v_gain.reshape(-1, 1, GM_W), w_sp,
      b_sp.reshape(-1, GM_GROUPS, GM_CHUNK, 1), mem_kv, w_out)


def _kv_proj_kernel(x_ref, g_ref, w_ref, k_ref, v_ref):
    h = _rms(x_ref[...], g_ref[...]).astype(BF16)
    kv = _dot(h, w_ref[...].astype(BF16))
    k_ref[...] = kv[:, :SB_W].astype(BF16)
    v_ref[...] = kv[:, SB_W:].astype(BF16)


def _kv_proj(x, g, w):
    t, d = x.shape
    out = pl.BlockSpec((TM, SB_W), lambda i: (i, 0))
    return pl.pallas_call(
        _kv_proj_kernel,
        grid=(t // TM,),
        in_specs=[pl.BlockSpec((TM, d), lambda i: (i, 0)), _full((1, d)), _full(w.shape)],
        out_specs=[out, out],
        out_shape=[jax.ShapeDtypeStruct((t, SB_W), BF16)] * 2,
        compiler_params=_params("parallel"),
        name="kv_proj",
    )(x, g.reshape(1, d), w)


def _b_in_kernel(x_ref, g_ref, w_ref, qsb_ref, qmem_ref):
    h = _rms(x_ref[...], g_ref[0]).astype(BF16)
    q = _dot(h, w_ref[0].astype(BF16))
    qsb_ref[...] = (q[:, :SB_W] * (QK_SCALE * LOG2E)).astype(BF16)
    qmem_ref[...] = (q[:, SB_W:] * QK_SCALE).astype(BF16)


def _b_in(x, g, w, layer, w_layer):
    t, d = x.shape
    return pl.pallas_call(
        _b_in_kernel,
        grid=(t // TM,),
        in_specs=[pl.BlockSpec((TM, d), lambda i: (i, 0)), _layer_block((1, d), layer),
                  _layer_block(w.shape[1:], w_layer)],
        out_specs=[pl.BlockSpec((TM, SB_W), lambda i: (i, 0)), pl.BlockSpec((TM, MEM_W), lambda i: (i, 0))],
        out_shape=[jax.ShapeDtypeStruct((t, SB_W), BF16), jax.ShapeDtypeStruct((t, MEM_W), BF16)],
        compiler_params=_params("parallel"),
        name="b_in",
    )(x, g.reshape(-1, 1, d), w)


def _b_out_kernel(x_ref, ytok_ref, qmem_ref, kv_ref, wout_ref, o_ref, y_ref):
    y_ref[:, :SB_W] = ytok_ref[...]
    _memory_attention(qmem_ref[...], kv_ref.at[0], y_ref, SB_W)
    o_ref[...] = x_ref[...] + _dot(y_ref[...], wout_ref[0].astype(BF16))


def _b_out(x, y_tok, q_mem, mem_kv, w_out, layer, w_layer, batch):
    t, d = x.shape
    row = pl.BlockSpec((TM, d), lambda i: (i, 0))
    return pl.pallas_call(
        _b_out_kernel,
        grid=(t // TM,),
        in_specs=[
            row,
            pl.BlockSpec((TM, SB_W), lambda i: (i, 0)),
            pl.BlockSpec((TM, MEM_W), lambda i: (i, 0)),
            _mem_kv_block(layer, batch, t // batch),
            _layer_block(w_out.shape[1:], w_layer),
        ],
        out_specs=row,
        out_shape=jax.ShapeDtypeStruct((t, d), F32),
        scratch_shapes=[pltpu.VMEM((TM, d), BF16)],
        compiler_params=_params("parallel"),
        name="b_out",
    )(x, y_tok, q_mem, mem_kv, w_out)


def _softplus2(z):
    return jnp.maximum(z, 0.0) + jnp.log2(1.0 + jnp.exp2(-jnp.abs(z)))


def _sb_attn_kernel(q_ref, k_ref, v_ref, o_ref):
    i = pl.program_id(1)
    lane = lax.broadcasted_iota(jnp.int32, (TQ, LANES), 1)
    low = lane < HEAD_DIM
    kj = lax.broadcasted_iota(jnp.int32, (TK, TK), 0)
    ks = lax.broadcasted_iota(jnp.int32, (TK, TK), 1)
    suffix_ones = (kj >= ks).astype(BF16)

    qhs = []
    for p in range(HEAD_PAIRS_SB):
        q2 = q_ref[0, :, p * LANES:(p + 1) * LANES]
        qhs += [jnp.where(keep, q2, jnp.zeros_like(q2)) for keep in (low, ~low)]
    n_chains = 2 * HEAD_PAIRS_SB

    half = TQ // 2
    diag_pieces = []
    for r0, r1, k1 in ((0, half, half), (half, TQ, TK)):
        row = lax.broadcasted_iota(jnp.int32, (r1 - r0, k1), 0) + r0
        col = lax.broadcasted_iota(jnp.int32, (r1 - r0, k1), 1)
        diag_pieces.append((r0, r1, 0, k1, col < row))
    prev_pieces = [(0, half, 0, TK, None), (half, TQ, half, TK, None)]
    row = lax.broadcasted_iota(jnp.int32, (TQ, TK), 0)
    col = lax.broadcasted_iota(jnp.int32, (TQ, TK), 1)
    left_out = jnp.logical_and(row >= half, col < half)

    def scores(j, pieces):
        start = pl.multiple_of(j * TK, TK)
        kts = [k_ref[0, pl.ds(start, TK), p * LANES:(p + 1) * LANES] for p in range(HEAD_PAIRS_SB)]
        vts = [v_ref[0, pl.ds(start, TK), p * LANES:(p + 1) * LANES] for p in range(HEAD_PAIRS_SB)]
        out = []
        for r0, r1, k0, k1, mask in pieces:
            zs = [lax.dot_general(qhs[ch][r0:r1], kts[ch // 2][k0:k1], _NT, preferred_element_type=F32)
                  for ch in range(n_chains)]
            sps = [_softplus2(z) for z in zs]
            if mask is not None:
                sps = [jnp.where(mask, sp, 0.0) for sp in sps]
            sufs = [_dot(sp.astype(BF16), suffix_ones[:k1 - k0, :k1 - k0]) for sp in sps]
            out.append((zs, sufs, [vts[ch // 2][k0:k1] for ch in range(n_chains)]))
        return out

    def accumulate(tile, pieces, cs, accs):
        new_cs, new_accs = [[] for _ in range(n_chains)], [[] for _ in range(n_chains)]
        covered = 0
        for (r0, r1, _, _, mask), (zs, sufs, vts) in zip(pieces, tile):
            ws = [jnp.exp2(z - suf - c[r0:r1]) for z, suf, c in zip(zs, sufs, cs)]
            if mask is not None:
                ws = [jnp.where(mask, w, 0.0) for w in ws]
            for ch in range(n_chains):
                if r0 > covered:
                    new_accs[ch].append(jnp.zeros((r0 - covered, LANES), F32))
                    new_cs[ch].append(jnp.zeros((r0 - covered, 1), F32))
                new_accs[ch].append(_dot(ws[ch].astype(BF16), vts[ch]))
                new_cs[ch].append(sufs[ch][:, :1])
            covered = r1
        assert covered == TQ
        accs = tuple(acc + jnp.concatenate(parts, axis=0) for acc, parts in zip(accs, new_accs))
        cs = tuple(c + jnp.concatenate(parts, axis=0) for c, parts in zip(cs, new_cs))
        return cs, accs

    def c_min_of(cs):
        return functools.reduce(jnp.minimum, [jnp.min(c) for c in cs])

    def first_pass():
        diag = scores(i, diag_pieces)
        prev = scores(jnp.maximum(i - 1, 0), prev_pieces)
        cs, accs = accumulate(diag, diag_pieces, (jnp.zeros((TQ, 1), F32),) * n_chains,
                              (jnp.zeros((TQ, LANES), F32),) * n_chains)
        cs = tuple(jnp.where(i > 0, c, SB_NO_TILE) for c in cs)
        return accumulate(prev, prev_pieces, cs, accs)

    def write(accs):
        for p in range(HEAD_PAIRS_SB):
            o_ref[0, :, p * LANES:(p + 1) * LANES] = jnp.where(low, accs[2 * p], accs[2 * p + 1]).astype(BF16)

    cs, accs = first_pass()
    write(accs)

    @pl.when(jnp.logical_and(i > 0, c_min_of(cs) < SB_DONE))
    def _():
        def keep_going(carry):
            n, c_min, _, _ = carry
            return jnp.logical_and(n < i, c_min < SB_DONE)

        def step(carry):
            n, _, cs, accs = carry
            pieces = [(0, TQ, 0, TK, jnp.logical_or(n > 0, left_out))]
            cs, accs = accumulate(scores(i - 1 - n, pieces), pieces, cs, accs)
            return n + 1, c_min_of(cs), cs, accs

        cs, accs = first_pass()
        write(lax.while_loop(keep_going, step, (jnp.int32(0), c_min_of(cs), cs, accs))[3])


def _sb_attn(q, k, v):
    b, s, _ = q.shape
    qspec = pl.BlockSpec((1, TQ, SB_W), lambda bi, i: (bi, i, 0))
    kvspec = pl.BlockSpec((1, s, SB_W), lambda bi, i: (bi, 0, 0), pipeline_mode=pl.Buffered(1))
    return pl.pallas_call(
        _sb_attn_kernel,
        grid=(b, s // TQ),
        in_specs=[qspec, kvspec, kvspec],
        out_specs=qspec,
        out_shape=jax.ShapeDtypeStruct((b, s, SB_W), BF16),
        compiler_params=_params("parallel", "arbitrary"),
        name="sb_attn",
    )(q, k, v)


def kernel(x, mem, ffn1_norm, ffn1_w_gate, ffn1_w_up, ffn1_w_down, mix_norm, ffn2_norm, ffn2_w_gate, ffn2_w_up, ffn2_w_down, mem_norm, w_mem_kv, a_w_in, a_v_norm, a_w_spatial, a_b_spatial, a_w_out, kv_norm, w_kv, b_w_in, b_w_out, final_norm):
    b, s, d = x.shape
    assert d == D_MODEL and s % TM == 0 and s % TM_FFN == 0 and s % TQ == 0 and TQ == TK
    t = b * s
    xt = x.reshape(t, d)
    mem_kv = _mem_kv(mem.reshape(b * N_MEM, d), mem_norm, w_mem_kv)
    mem_kv = mem_kv.reshape(DEPTH * b, N_MEM, 2 * MEM_W)
    k = v = None
    for l in range(DEPTH):
        if l == N_A:
            k, v = _kv_proj(xt, kv_norm, w_kv)
            k = k.reshape(b, s, SB_W)
            v = v.reshape(b, s, SB_W)
        xt = _ffn(xt, ffn1_norm, ffn1_w_gate, ffn1_w_up, ffn1_w_down, l)
        if l < N_A:
            xt = _mixer_a(xt, mix_norm, a_w_in, a_v_norm, a_w_spatial, a_b_spatial, mem_kv, a_w_out, l, b)
        else:
            q_sb, q_mem = _b_in(xt, mix_norm, b_w_in, l, l - N_A)
            y_tok = _sb_attn(q_sb.reshape(b, s, SB_W), k, v).reshape(t, SB_W)
            xt = _b_out(xt, y_tok, q_mem, mem_kv, b_w_out, l, l - N_A, b)
        xt = _ffn(xt, ffn2_norm, ffn2_w_gate, ffn2_w_up, ffn2_w_down, l,
                  final_gain=final_norm if l == DEPTH - 1 else None)
    return xt.reshape(b, s, d)
```

```python
import functools

import jax
import jax.numpy as jnp
from jax import lax
from jax.experimental import pallas as pl
from jax.experimental.pallas import tpu as pltpu

D_MODEL = 1024
DEPTH = 4
N_A = DEPTH // 2
CHUNK = 64
N_MEM = 256
MEM_HEADS = 4
HEAD_DIM = 64
MEM_W = MEM_HEADS * HEAD_DIM
GM_W = D_MODEL - MEM_W
GM_GROUPS = 6
GM_GC = GM_W // GM_GROUPS
GM_CHUNK = 128
SB_W = D_MODEL - MEM_W
D_FF = 2816
EPS = 1e-6

LANES = 128
HEAD_PAIRS_SB = SB_W // LANES
HEAD_PAIRS_MEM = MEM_W // LANES
QK_SCALE = HEAD_DIM ** -0.5
LOG2E = 1.4426950408889634

TM = 1024
TM_FFN = 1024
FF_CHUNK = 256
TQ = 256
TK = 256
SB_DONE = 158.0
SB_NO_TILE = 1e30
VMEM_LIMIT = 60 * 1024 * 1024

BF16 = jnp.bfloat16
F32 = jnp.float32
_NT = (((1,), (1,)), ((), ()))


def _params(*sem):
    return pltpu.CompilerParams(dimension_semantics=sem, vmem_limit_bytes=VMEM_LIMIT)


def _rms(x, g):
    r = lax.rsqrt(jnp.mean(x * x, axis=-1, keepdims=True) + EPS)
    return x * r * g


def _dot(a, b):
    return jnp.dot(a, b, preferred_element_type=F32)


def _full(shape):
    return pl.BlockSpec(shape, lambda *_: (0,) * len(shape))


def _layer_block(shape, layer):
    return pl.BlockSpec((1,) + tuple(shape), lambda *_: (layer,) + (0,) * len(shape),
                        pipeline_mode=pl.Buffered(1))


def _mem_kv_block(layer, batch, seq):
    steps_per_batch = seq // TM
    return pl.BlockSpec((1, N_MEM, 2 * MEM_W), lambda i: (layer * batch + i // steps_per_batch, 0, 0))


def _ffn_kernel(x_ref, g_ref, wg_ref, wu_ref, wd_ref, *rest, final):
    if final:
        gf_ref, o_ref = rest
    else:
        o_ref, = rest
    h = _rms(x_ref[...], g_ref[0]).astype(BF16)
    for c in range(D_FF // FF_CHUNK):
        cols = slice(c * FF_CHUNK, (c + 1) * FF_CHUNK)
        gate = _dot(h, wg_ref[0, :, cols].astype(BF16))
        up = _dot(h, wu_ref[0, :, cols].astype(BF16))
        act = (gate * jax.nn.sigmoid(gate) * up).astype(BF16)
        part = _dot(act, wd_ref[0, cols, :].astype(BF16))
        if c == 0:
            o_ref[...] = part
        else:
            o_ref[...] += part
    y = x_ref[...] + 0.5 * o_ref[...]
    if final:
        y = _rms(y, gf_ref[...])
    o_ref[...] = y


def _ffn(x, g, wg, wu, wd, layer, final_gain=None):
    t, d = x.shape
    final = final_gain is not None
    row = pl.BlockSpec((TM_FFN, d), lambda i: (i, 0))
    in_specs = [row, _layer_block((1, d), layer), _layer_block((d, D_FF), layer),
                _layer_block((d, D_FF), layer), _layer_block((D_FF, d), layer)]
    args = [x, g.reshape(-1, 1, d), wg, wu, wd]
    if final:
        in_specs.append(_full((1, d)))
        args.append(final_gain.reshape(1, d))
    return pl.pallas_call(
        functools.partial(_ffn_kernel, final=final),
        grid=(t // TM_FFN,),
        in_specs=in_specs,
        out_specs=row,
        out_shape=jax.ShapeDtypeStruct((t, d), F32),
        compiler_params=_params("parallel"),
        name="ffn_final" if final else "ffn",
    )(*args)


def _mem_kv_kernel(mem_ref, g_ref, w_ref, o_ref):
    h = _rms(mem_ref[...], g_ref[...]).astype(BF16)
    o_ref[0] = _dot(h, w_ref[0].astype(BF16)).astype(BF16)


def _mem_kv(mem, g, w):
    n, d = mem.shape
    depth, _, width = w.shape
    return pl.pallas_call(
        _mem_kv_kernel,
        grid=(depth,),
        in_specs=[_full((n, d)), _full((1, d)), pl.BlockSpec((1, d, width), lambda l: (l, 0, 0))],
        out_specs=pl.BlockSpec((1, n, width), lambda l: (l, 0, 0)),
        out_shape=jax.ShapeDtypeStruct((depth, n, width), BF16),
        compiler_params=_params("parallel"),
        name="mem_kv",
    )(mem, g.reshape(1, d), w)


def _memory_attention(q_mem, kv_ref, y_ref, col0):
    rows = q_mem.shape[0]
    lane = lax.broadcasted_iota(jnp.int32, (rows, LANES), 1)
    low = lane < HEAD_DIM
    for p in range(HEAD_PAIRS_MEM):
        cols = slice(p * LANES, (p + 1) * LANES)
        q2 = q_mem[:, cols]
        k2 = kv_ref[:, cols]
        v2 = kv_ref[:, MEM_W + p * LANES:MEM_W + (p + 1) * LANES]
        outs = []
        for keep in (low, ~low):
            s = lax.dot_general(jnp.where(keep, q2, jnp.zeros_like(q2)), k2, _NT,
                                preferred_element_type=F32)
            e = jnp.exp(s - jnp.max(s, axis=-1, keepdims=True))
            inv = 1.0 / jnp.sum(e, axis=-1, keepdims=True)
            outs.append(_dot(e.astype(BF16), v2) * inv)
        y_ref[:, col0 + p * LANES:col0 + (p + 1) * LANES] = jnp.where(low, outs[0], outs[1]).astype(BF16)


def _mixer_a_kernel(x_ref, g_ref, win_ref, vg_ref, ws_ref, bs_ref, kv_ref, wout_ref, o_ref, y_ref):
    x = x_ref[...]
    h = _rms(x, g_ref[0]).astype(BF16)
    proj = _dot(h, win_ref[0].astype(BF16))
    u = jax.nn.gelu(proj[:, :GM_W])
    v = _rms(jax.nn.gelu(proj[:, GM_W:2 * GM_W]), vg_ref[0]).astype(BF16)
    pos_t = lax.broadcasted_iota(jnp.int32, (GM_CHUNK, GM_CHUNK), 0) // CHUNK
    pos_s = lax.broadcasted_iota(jnp.int32, (GM_CHUNK, GM_CHUNK), 1) // CHUNK
    chunk_causal = pos_s <= pos_t
    for gi in range(GM_GROUPS):
        cols = slice(gi * GM_GC, (gi + 1) * GM_GC)
        w = jnp.where(chunk_causal, ws_ref[0, gi], 0.0).astype(BF16)
        bias = bs_ref[0, gi]
        for n in range(TM // GM_CHUNK):
            rows = slice(n * GM_CHUNK, (n + 1) * GM_CHUNK)
            mixed = _dot(w, v[rows, cols]) + bias
            y_ref[rows, cols] = (u[rows, cols] * mixed).astype(BF16)
    q_mem = (proj[:, 2 * GM_W:] * QK_SCALE).astype(BF16)
    _memory_attention(q_mem, kv_ref.at[0], y_ref, GM_W)
    o_ref[...] = x + _dot(y_ref[...], wout_ref[0].astype(BF16))


def _mixer_a(x, g, w_in, v_gain, w_sp, b_sp, mem_kv, w_out, layer, batch):
    t, d = x.shape
    row = pl.BlockSpec((TM, d), lambda i: (i, 0))
    return pl.pallas_call(
        _mixer_a_kernel,
        grid=(t // TM,),
        in_specs=[
            row, _layer_block((1, d), layer), _layer_block(w_in.shape[1:], layer),
            _layer_block((1, GM_W), layer), _layer_block(w_sp.shape[1:], layer),
            _layer_block((GM_GROUPS, GM_CHUNK, 1), layer),
            _mem_kv_block(layer, batch, t // batch),
            _layer_block(w_out.shape[1:], layer),
        ],
        out_specs=row,
        out_shape=jax.ShapeDtypeStruct((t, d), F32),
        scratch_shapes=[pltpu.VMEM((TM, d), BF16)],
        compiler_params=_params("parallel"),
        name="mixer_a",
    )(x, g.reshape(-1, 1, d), w_in, v_gain.reshape(-1, 1, GM_W), w_sp,
      b_sp.reshape(-1, GM_GROUPS, GM_CHUNK, 1), mem_kv, w_out)


def _kv_proj_kernel(x_ref, g_ref, w_ref, k_ref, v_ref):
    h = _rms(x_ref[...], g_ref[...]).astype(BF16)
    kv = _dot(h, w_ref[...].astype(BF16))
    k_ref[...] = kv[:, :SB_W].astype(BF16)
    v_ref[...] = kv[:, SB_W:].astype(BF16)


def _kv_proj(x, g, w):
    t, d = x.shape
    out = pl.BlockSpec((TM, SB_W), lambda i: (i, 0))
    return pl.pallas_call(
        _kv_proj_kernel,
        grid=(t // TM,),
        in_specs=[pl.BlockSpec((TM, d), lambda i: (i, 0)), _full((1, d)), _full(w.shape)],
        out_specs=[out, out],
        out_shape=[jax.ShapeDtypeStruct((t, SB_W), BF16)] * 2,
        compiler_params=_params("parallel"),
        name="kv_proj",
    )(x, g.reshape(1, d), w)


def _b_in_kernel(x_ref, g_ref, w_ref, qsb_ref, qmem_ref):
    h = _rms(x_ref[...], g_ref[0]).astype(BF16)
    q = _dot(h, w_ref[0].astype(BF16))
    qsb_ref[...] = (q[:, :SB_W] * (QK_SCALE * LOG2E)).astype(BF16)
    qmem_ref[...] = (q[:, SB_W:] * QK_SCALE).astype(BF16)


def _b_in(x, g, w, layer, w_layer):
    t, d = x.shape
    return pl.pallas_call(
        _b_in_kernel,
        grid=(t // TM,),
        in_specs=[pl.BlockSpec((TM, d), lambda i: (i, 0)), _layer_block((1, d), layer),
                  _layer_block(w.shape[1:], w_layer)],
        out_specs=[pl.BlockSpec((TM, SB_W), lambda i: (i, 0)), pl.BlockSpec((TM, MEM_W), lambda i: (i, 0))],
        out_shape=[jax.ShapeDtypeStruct((t, SB_W), BF16), jax.ShapeDtypeStruct((t, MEM_W), BF16)],
        compiler_params=_params("parallel"),
        name="b_in",
    )(x, g.reshape(-1, 1, d), w)


def _b_out_kernel(x_ref, ytok_ref, qmem_ref, kv_ref, wout_ref, o_ref, y_ref):
    y_ref[:, :SB_W] = ytok_ref[...]
    _memory_attention(qmem_ref[...], kv_ref.at[0], y_ref, SB_W)
    o_ref[...] = x_ref[...] + _dot(y_ref[...], wout_ref[0].astype(BF16))


def _b_out(x, y_tok, q_mem, mem_kv, w_out, layer, w_layer, batch):
    t, d = x.shape
    row = pl.BlockSpec((TM, d), lambda i: (i, 0))
    return pl.pallas_call(
        _b_out_kernel,
        grid=(t // TM,),
        in_specs=[
            row,
            pl.BlockSpec((TM, SB_W), lambda i: (i, 0)),
            pl.BlockSpec((TM, MEM_W), lambda i: (i, 0)),
            _mem_kv_block(layer, batch, t // batch),
            _layer_block(w_out.shape[1:], w_layer),
        ],
        out_specs=row,
        out_shape=jax.ShapeDtypeStruct((t, d), F32),
        scratch_shapes=[pltpu.VMEM((TM, d), BF16)],
        compiler_params=_params("parallel"),
        name="b_out",
    )(x, y_tok, q_mem, mem_kv, w_out)


def _softplus2(z):
    return jnp.maximum(z, 0.0) + jnp.log2(1.0 + jnp.exp2(-jnp.abs(z)))


def _sb_attn_kernel(q_ref, kd_ref, kp_ref, vd_ref, vp_ref, k_hbm, v_hbm, o_ref, k_buf, v_buf, sem):
    i = pl.program_id(1)
    lane = lax.broadcasted_iota(jnp.int32, (TQ, LANES), 1)
    low = lane < HEAD_DIM
    kj = lax.broadcasted_iota(jnp.int32, (TK, TK), 0)
    ks = lax.broadcasted_iota(jnp.int32, (TK, TK), 1)
    suffix_ones = (kj >= ks).astype(BF16)

    qhs = []
    for p in range(HEAD_PAIRS_SB):
        q2 = q_ref[0, :, p * LANES:(p + 1) * LANES]
        qhs += [jnp.where(keep, q2, jnp.zeros_like(q2)) for keep in (low, ~low)]
    n_chains = 2 * HEAD_PAIRS_SB

    half = TQ // 2
    diag_pieces = []
    for r0, r1, k1 in ((0, half, half), (half, TQ, TK)):
        row = lax.broadcasted_iota(jnp.int32, (r1 - r0, k1), 0) + r0
        col = lax.broadcasted_iota(jnp.int32, (r1 - r0, k1), 1)
        diag_pieces.append((r0, r1, 0, k1, col < row))
    prev_pieces = [(0, half, 0, TK, None), (half, TQ, half, TK, None)]
    row = lax.broadcasted_iota(jnp.int32, (TQ, TK), 0)
    col = lax.broadcasted_iota(jnp.int32, (TQ, TK), 1)
    left_out = jnp.logical_and(row >= half, col < half)

    def scores(k_tile, v_tile, pieces):
        kts = [k_tile[:, p * LANES:(p + 1) * LANES] for p in range(HEAD_PAIRS_SB)]
        vts = [v_tile[:, p * LANES:(p + 1) * LANES] for p in range(HEAD_PAIRS_SB)]
        out = []
        for r0, r1, k0, k1, mask in pieces:
            zs = [lax.dot_general(qhs[ch][r0:r1], kts[ch // 2][k0:k1], _NT, preferred_element_type=F32)
                  for ch in range(n_chains)]
            sps = [_softplus2(z) for z in zs]
            if mask is not None:
                sps = [jnp.where(mask, sp, 0.0) for sp in sps]
            sufs = [_dot(sp.astype(BF16), suffix_ones[:k1 - k0, :k1 - k0]) for sp in sps]
            out.append((zs, sufs, [vts[ch // 2][k0:k1] for ch in range(n_chains)]))
        return out

    def accumulate(tile, pieces, cs, accs):
        new_cs, new_accs = [[] for _ in range(n_chains)], [[] for _ in range(n_chains)]
        covered = 0
        for (r0, r1, _, _, mask), (zs, sufs, vts) in zip(pieces, tile):
            ws = [jnp.exp2(z - suf - c[r0:r1]) for z, suf, c in zip(zs, sufs, cs)]
            if mask is not None:
                ws = [jnp.where(mask, w, 0.0) for w in ws]
            for ch in range(n_chains):
                if r0 > covered:
                    new_accs[ch].append(jnp.zeros((r0 - covered, LANES), F32))
                    new_cs[ch].append(jnp.zeros((r0 - covered, 1), F32))
                new_accs[ch].append(_dot(ws[ch].astype(BF16), vts[ch]))
                new_cs[ch].append(sufs[ch][:, :1])
            covered = r1
        assert covered == TQ
        accs = tuple(acc + jnp.concatenate(parts, axis=0) for acc, parts in zip(accs, new_accs))
        cs = tuple(c + jnp.concatenate(parts, axis=0) for c, parts in zip(cs, new_cs))
        return cs, accs

    def c_min_of(cs):
        return functools.reduce(jnp.minimum, [jnp.min(c) for c in cs])

    def first_pass():
        diag = scores(kd_ref.at[0], vd_ref.at[0], diag_pieces)
        prev = scores(kp_ref.at[0], vp_ref.at[0], prev_pieces)
        cs, accs = accumulate(diag, diag_pieces, (jnp.zeros((TQ, 1), F32),) * n_chains,
                              (jnp.zeros((TQ, LANES), F32),) * n_chains)
        cs = tuple(jnp.where(i > 0, c, SB_NO_TILE) for c in cs)
        return accumulate(prev, prev_pieces, cs, accs)

    def write(accs):
        for p in range(HEAD_PAIRS_SB):
            o_ref[0, :, p * LANES:(p + 1) * LANES] = jnp.where(low, accs[2 * p], accs[2 * p + 1]).astype(BF16)

    cs, accs = first_pass()
    write(accs)

    @pl.when(jnp.logical_and(i > 0, c_min_of(cs) < SB_DONE))
    def _():
        def keep_going(carry):
            n, c_min, _, _ = carry
            return jnp.logical_and(n < i, c_min < SB_DONE)

        def step(carry):
            n, _, cs, accs = carry
            pieces = [(0, TQ, 0, TK, jnp.logical_or(n > 0, left_out))]
            keys = pl.ds(pl.multiple_of((i - 1 - n) * TK, TK), TK)
            copies = (pltpu.make_async_copy(k_hbm.at[pl.program_id(0), keys, :], k_buf, sem.at[0]),
                      pltpu.make_async_copy(v_hbm.at[pl.program_id(0), keys, :], v_buf, sem.at[1]))
            for copy in copies:
                copy.start()
            for copy in copies:
                copy.wait()
            cs, accs = accumulate(scores(k_buf, v_buf, pieces), pieces, cs, accs)
            return n + 1, c_min_of(cs), cs, accs

        cs, accs = first_pass()
        write(lax.while_loop(keep_going, step, (jnp.int32(0), c_min_of(cs), cs, accs))[3])


def _sb_attn(q, k, v):
    b, s, _ = q.shape
    qspec = pl.BlockSpec((1, TQ, SB_W), lambda bi, i: (bi, i, 0))
    before = pl.BlockSpec((1, TK, SB_W), lambda bi, i: (bi, jnp.maximum(i - 1, 0), 0))
    hbm = pl.BlockSpec(memory_space=pl.ANY)
    return pl.pallas_call(
        _sb_attn_kernel,
        grid=(b, s // TQ),
        in_specs=[qspec, qspec, before, qspec, before, hbm, hbm],
        out_specs=qspec,
        out_shape=jax.ShapeDtypeStruct((b, s, SB_W), BF16),
        scratch_shapes=[pltpu.VMEM((TK, SB_W), BF16), pltpu.VMEM((TK, SB_W), BF16),
                        pltpu.SemaphoreType.DMA((2,))],
        compiler_params=_params("parallel", "arbitrary"),
        name="sb_attn",
    )(q, k, k, v, v, k, v)


def kernel(x, mem, ffn1_norm, ffn1_w_gate, ffn1_w_up, ffn1_w_down, mix_norm, ffn2_norm, ffn2_w_gate, ffn2_w_up, ffn2_w_down, mem_norm, w_mem_kv, a_w_in, a_v_norm, a_w_spatial, a_b_spatial, a_w_out, kv_norm, w_kv, b_w_in, b_w_out, final_norm):
    b, s, d = x.shape
    assert d == D_MODEL and s % TM == 0 and s % TM_FFN == 0 and s % TQ == 0 and TQ == TK
    t = b * s
    xt = x.reshape(t, d)
    mem_kv = _mem_kv(mem.reshape(b * N_MEM, d), mem_norm, w_mem_kv)
    mem_kv = mem_kv.reshape(DEPTH * b, N_MEM, 2 * MEM_W)
    k = v = None
    for l in range(DEPTH):
        if l == N_A:
            k, v = _kv_proj(xt, kv_norm, w_kv)
            k = k.reshape(b, s, SB_W)
            v = v.reshape(b, s, SB_W)
        xt = _ffn(xt, ffn1_norm, ffn1_w_gate, ffn1_w_up, ffn1_w_down, l)
        if l < N_A:
            xt = _mixer_a(xt, mix_norm, a_w_in, a_v_norm, a_w_spatial, a_b_spatial, mem_kv, a_w_out, l, b)
        else:
            q_sb, q_mem = _b_in(xt, mix_norm, b_w_in, l, l - N_A)
            y_tok = _sb_attn(q_sb.reshape(b, s, SB_W), k, v).reshape(t, SB_W)
            xt = _b_out(xt, y_tok, q_mem, mem_kv, b_w_out, l, l - N_A, b)
        xt = _ffn(xt, ffn2_norm, ffn2_w_gate, ffn2_w_up, ffn2_w_down, l,
                  final_gain=final_norm if l == DEPTH - 1 else None)
    return xt.reshape(b, s, d)
```

```python
import functools

import jax
import jax.numpy as jnp
from jax import lax
from jax.experimental import pallas as pl
from jax.experimental.pallas import tpu as pltpu

D_MODEL = 1024
DEPTH = 4
N_A = DEPTH // 2
CHUNK = 64
N_MEM = 256
MEM_HEADS = 4
HEAD_DIM = 64
MEM_W = MEM_HEADS * HEAD_DIM
GM_W = D_MODEL - MEM_W
GM_GROUPS = 6
GM_GC = GM_W // GM_GROUPS
GM_CHUNK = 128
SB_W = D_MODEL - MEM_W
D_FF = 2816
EPS = 1e-6

LANES = 128
HEAD_PAIRS_SB = SB_W // LANES
HEAD_PAIRS_MEM = MEM_W // LANES
QK_SCALE = HEAD_DIM ** -0.5
LOG2E = 1.4426950408889634

TM = 1024
TM_FFN = 1024
KV_RING = 3
FF_CHUNK = 256
TQ = 256
TK = 256
SB_DONE = 158.0
SB_NO_TILE = 1e30
VMEM_LIMIT = 60 * 1024 * 1024

BF16 = jnp.bfloat16
F32 = jnp.float32
_NT = (((1,), (1,)), ((), ()))


def _params(*sem):
    return pltpu.CompilerParams(dimension_semantics=sem, vmem_limit_bytes=VMEM_LIMIT)


def _rms(x, g):
    r = lax.rsqrt(jnp.mean(x * x, axis=-1, keepdims=True) + EPS)
    return x * r * g


def _dot(a, b):
    return jnp.dot(a, b, preferred_element_type=F32)


def _full(shape):
    return pl.BlockSpec(shape, lambda *_: (0,) * len(shape))


def _layer_block(shape, layer):
    return pl.BlockSpec((1,) + tuple(shape), lambda *_: (layer,) + (0,) * len(shape),
                        pipeline_mode=pl.Buffered(1))


def _mem_kv_block(layer, batch, seq):
    steps_per_batch = seq // TM
    return pl.BlockSpec((1, N_MEM, 2 * MEM_W), lambda i: (layer * batch + i // steps_per_batch, 0, 0))


def _ffn_kernel(x_ref, g_ref, wg_ref, wu_ref, wd_ref, *rest, final):
    if final:
        gf_ref, o_ref = rest
    else:
        o_ref, = rest
    h = _rms(x_ref[...], g_ref[0]).astype(BF16)
    for c in range(D_FF // FF_CHUNK):
        cols = slice(c * FF_CHUNK, (c + 1) * FF_CHUNK)
        gate = _dot(h, wg_ref[0, :, cols].astype(BF16))
        up = _dot(h, wu_ref[0, :, cols].astype(BF16))
        act = (gate * jax.nn.sigmoid(gate) * up).astype(BF16)
        part = _dot(act, wd_ref[0, cols, :].astype(BF16))
        if c == 0:
            o_ref[...] = part
        else:
            o_ref[...] += part
    y = x_ref[...] + 0.5 * o_ref[...]
    if final:
        y = _rms(y, gf_ref[...])
    o_ref[...] = y


def _ffn(x, g, wg, wu, wd, layer, final_gain=None):
    t, d = x.shape
    final = final_gain is not None
    row = pl.BlockSpec((TM_FFN, d), lambda i: (i, 0))
    in_specs = [row, _layer_block((1, d), layer), _layer_block((d, D_FF), layer),
                _layer_block((d, D_FF), layer), _layer_block((D_FF, d), layer)]
    args = [x, g.reshape(-1, 1, d), wg, wu, wd]
    if final:
        in_specs.append(_full((1, d)))
        args.append(final_gain.reshape(1, d))
    return pl.pallas_call(
        functools.partial(_ffn_kernel, final=final),
        grid=(t // TM_FFN,),
        in_specs=in_specs,
        out_specs=row,
        out_shape=jax.ShapeDtypeStruct((t, d), F32),
        compiler_params=_params("parallel"),
        name="ffn_final" if final else "ffn",
    )(*args)


def _mem_kv_kernel(mem_ref, g_ref, w_ref, o_ref):
    h = _rms(mem_ref[...], g_ref[...]).astype(BF16)
    o_ref[0] = _dot(h, w_ref[0].astype(BF16)).astype(BF16)


def _mem_kv(mem, g, w):
    n, d = mem.shape
    depth, _, width = w.shape
    return pl.pallas_call(
        _mem_kv_kernel,
        grid=(depth,),
        in_specs=[_full((n, d)), _full((1, d)), pl.BlockSpec((1, d, width), lambda l: (l, 0, 0))],
        out_specs=pl.BlockSpec((1, n, width), lambda l: (l, 0, 0)),
        out_shape=jax.ShapeDtypeStruct((depth, n, width), BF16),
        compiler_params=_params("parallel"),
        name="mem_kv",
    )(mem, g.reshape(1, d), w)


def _memory_attention(q_mem, kv_ref, y_ref, col0):
    rows = q_mem.shape[0]
    lane = lax.broadcasted_iota(jnp.int32, (rows, LANES), 1)
    low = lane < HEAD_DIM
    for p in range(HEAD_PAIRS_MEM):
        cols = slice(p * LANES, (p + 1) * LANES)
        q2 = q_mem[:, cols]
        k2 = kv_ref[:, cols]
        v2 = kv_ref[:, MEM_W + p * LANES:MEM_W + (p + 1) * LANES]
        outs = []
        for keep in (low, ~low):
            s = lax.dot_general(jnp.where(keep, q2, jnp.zeros_like(q2)), k2, _NT,
                                preferred_element_type=F32)
            e = jnp.exp(s - jnp.max(s, axis=-1, keepdims=True))
            inv = 1.0 / jnp.sum(e, axis=-1, keepdims=True)
            outs.append(_dot(e.astype(BF16), v2) * inv)
        y_ref[:, col0 + p * LANES:col0 + (p + 1) * LANES] = jnp.where(low, outs[0], outs[1]).astype(BF16)


def _mixer_a_kernel(x_ref, g_ref, win_ref, vg_ref, ws_ref, bs_ref, kv_ref, wout_ref, o_ref, y_ref):
    x = x_ref[...]
    h = _rms(x, g_ref[0]).astype(BF16)
    proj = _dot(h, win_ref[0].astype(BF16))
    u = jax.nn.gelu(proj[:, :GM_W])
    v = _rms(jax.nn.gelu(proj[:, GM_W:2 * GM_W]), vg_ref[0]).astype(BF16)
    pos_t = lax.broadcasted_iota(jnp.int32, (GM_CHUNK, GM_CHUNK), 0) // CHUNK
    pos_s = lax.broadcasted_iota(jnp.int32, (GM_CHUNK, GM_CHUNK), 1) // CHUNK
    chunk_causal = pos_s <= pos_t
    for gi in range(GM_GROUPS):
        cols = slice(gi * GM_GC, (gi + 1) * GM_GC)
        w = jnp.where(chunk_causal, ws_ref[0, gi], 0.0).astype(BF16)
        bias = bs_ref[0, gi]
        for n in range(TM // GM_CHUNK):
            rows = slice(n * GM_CHUNK, (n + 1) * GM_CHUNK)
            mixed = _dot(w, v[rows, cols]) + bias
            y_ref[rows, cols] = (u[rows, cols] * mixed).astype(BF16)
    q_mem = (proj[:, 2 * GM_W:] * QK_SCALE).astype(BF16)
    _memory_attention(q_mem, kv_ref.at[0], y_ref, GM_W)
    o_ref[...] = x + _dot(y_ref[...], wout_ref[0].astype(BF16))


def _mixer_a(x, g, w_in, v_gain, w_sp, b_sp, mem_kv, w_out, layer, batch):
    t, d = x.shape
    row = pl.BlockSpec((TM, d), lambda i: (i, 0))
    return pl.pallas_call(
        _mixer_a_kernel,
        grid=(t // TM,),
        in_specs=[
            row, _layer_block((1, d), layer), _layer_block(w_in.shape[1:], layer),
            _layer_block((1, GM_W), layer), _layer_block(w_sp.shape[1:], layer),
            _layer_block((GM_GROUPS, GM_CHUNK, 1), layer),
            _mem_kv_block(layer, batch, t // batch),
            _layer_block(w_out.shape[1:], layer),
        ],
        out_specs=row,
        out_shape=jax.ShapeDtypeStruct((t, d), F32),
        scratch_shapes=[pltpu.VMEM((TM, d), BF16)],
        compiler_params=_params("parallel"),
        name="mixer_a",
    )(x, g.reshape(-1, 1, d), w_in, v_gain.reshape(-1, 1, GM_W), w_sp,
      b_sp.reshape(-1, GM_GROUPS, GM_CHUNK, 1), mem_kv, w_out)


def _kv_proj_kernel(x_hbm, g_ref, w_ref, k_ref, v_ref, x_ring, sem):
    s = pl.program_id(0)
    n = pl.num_programs(0)

    def tile_copy(tile, slot):
        return pltpu.make_async_copy(x_hbm.at[pl.ds(tile * TM, TM), :], x_ring.at[slot], sem.at[slot])

    @pl.when(s == 0)
    def _():
        for tile in range(KV_RING):
            tile_copy(tile, tile).start()

    slot = s % KV_RING
    tile_copy(s, slot).wait()
    h = _rms(x_ring[slot], g_ref[...]).astype(BF16)
    kv = _dot(h, w_ref[...].astype(BF16))
    k_ref[...] = kv[:, :SB_W].astype(BF16)
    v_ref[...] = kv[:, SB_W:].astype(BF16)

    @pl.when(s + KV_RING < n)
    def _():
        tile_copy(s + KV_RING, slot).start()


def _kv_proj(x, g, w):
    t, d = x.shape
    assert t // TM >= KV_RING
    out = pl.BlockSpec((TM, SB_W), lambda i: (i, 0))
    return pl.pallas_call(
        _kv_proj_kernel,
        grid=(t // TM,),
        in_specs=[pl.BlockSpec(memory_space=pl.ANY), _full((1, d)), _full(w.shape)],
        out_specs=[out, out],
        out_shape=[jax.ShapeDtypeStruct((t, SB_W), BF16)] * 2,
        scratch_shapes=[pltpu.VMEM((KV_RING, TM, d), F32), pltpu.SemaphoreType.DMA((KV_RING,))],
        compiler_params=_params("arbitrary"),
        name="kv_proj",
    )(x, g.reshape(1, d), w)


def _b_in_kernel(x_ref, g_ref, w_ref, qsb_ref, qmem_ref):
    h = _rms(x_ref[...], g_ref[0]).astype(BF16)
    q = _dot(h, w_ref[0].astype(BF16))
    qsb_ref[...] = (q[:, :SB_W] * (QK_SCALE * LOG2E)).astype(BF16)
    qmem_ref[...] = (q[:, SB_W:] * QK_SCALE).astype(BF16)


def _b_in(x, g, w, layer, w_layer):
    t, d = x.shape
    return pl.pallas_call(
        _b_in_kernel,
        grid=(t // TM,),
        in_specs=[pl.BlockSpec((TM, d), lambda i: (i, 0)), _layer_block((1, d), layer),
                  _layer_block(w.shape[1:], w_layer)],
        out_specs=[pl.BlockSpec((TM, SB_W), lambda i: (i, 0)), pl.BlockSpec((TM, MEM_W), lambda i: (i, 0))],
        out_shape=[jax.ShapeDtypeStruct((t, SB_W), BF16), jax.ShapeDtypeStruct((t, MEM_W), BF16)],
        compiler_params=_params("parallel"),
        name="b_in",
    )(x, g.reshape(-1, 1, d), w)


def _b_out_kernel(x_ref, ytok_ref, qmem_ref, kv_ref, wout_ref, o_ref, y_ref):
    y_ref[:, :SB_W] = ytok_ref[...]
    _memory_attention(qmem_ref[...], kv_ref.at[0], y_ref, SB_W)
    o_ref[...] = x_ref[...] + _dot(y_ref[...], wout_ref[0].astype(BF16))


def _b_out(x, y_tok, q_mem, mem_kv, w_out, layer, w_layer, batch):
    t, d = x.shape
    row = pl.BlockSpec((TM, d), lambda i: (i, 0))
    return pl.pallas_call(
        _b_out_kernel,
        grid=(t // TM,),
        in_specs=[
            row,
            pl.BlockSpec((TM, SB_W), lambda i: (i, 0)),
            pl.BlockSpec((TM, MEM_W), lambda i: (i, 0)),
            _mem_kv_block(layer, batch, t // batch),
            _layer_block(w_out.shape[1:], w_layer),
        ],
        out_specs=row,
        out_shape=jax.ShapeDtypeStruct((t, d), F32),
        scratch_shapes=[pltpu.VMEM((TM, d), BF16)],
        compiler_params=_params("parallel"),
        name="b_out",
    )(x, y_tok, q_mem, mem_kv, w_out)


def _softplus2(z):
    return jnp.maximum(z, 0.0) + jnp.log2(1.0 + jnp.exp2(-jnp.abs(z)))


def _sb_attn_kernel(q_ref, kd_ref, kp_ref, vd_ref, vp_ref, k_hbm, v_hbm, o_ref, k_buf, v_buf, sem):
    i = pl.program_id(1)
    lane = lax.broadcasted_iota(jnp.int32, (TQ, LANES), 1)
    low = lane < HEAD_DIM
    kj = lax.broadcasted_iota(jnp.int32, (TK, TK), 0)
    ks = lax.broadcasted_iota(jnp.int32, (TK, TK), 1)
    suffix_ones = (kj >= ks).astype(BF16)

    qhs = []
    for p in range(HEAD_PAIRS_SB):
        q2 = q_ref[0, :, p * LANES:(p + 1) * LANES]
        qhs += [jnp.where(keep, q2, jnp.zeros_like(q2)) for keep in (low, ~low)]
    n_chains = 2 * HEAD_PAIRS_SB

    half = TQ // 2
    diag_pieces = []
    for r0, r1, k1 in ((0, half, half), (half, TQ, TK)):
        row = lax.broadcasted_iota(jnp.int32, (r1 - r0, k1), 0) + r0
        col = lax.broadcasted_iota(jnp.int32, (r1 - r0, k1), 1)
        diag_pieces.append((r0, r1, 0, k1, col < row))
    prev_pieces = [(0, half, 0, TK, None), (half, TQ, half, TK, None)]
    row = lax.broadcasted_iota(jnp.int32, (TQ, TK), 0)
    col = lax.broadcasted_iota(jnp.int32, (TQ, TK), 1)
    left_out = jnp.logical_and(row >= half, col < half)

    def scores(k_tile, v_tile, pieces):
        kts = [k_tile[:, p * LANES:(p + 1) * LANES] for p in range(HEAD_PAIRS_SB)]
        vts = [v_tile[:, p * LANES:(p + 1) * LANES] for p in range(HEAD_PAIRS_SB)]
        out = []
        for r0, r1, k0, k1, mask in pieces:
            zs = [lax.dot_general(qhs[ch][r0:r1], kts[ch // 2][k0:k1], _NT, preferred_element_type=F32)
                  for ch in range(n_chains)]
            sps = [_softplus2(z) for z in zs]
            if mask is not None:
                sps = [jnp.where(mask, sp, 0.0) for sp in sps]
            sufs = [_dot(sp.astype(BF16), suffix_ones[:k1 - k0, :k1 - k0]) for sp in sps]
            out.append((zs, sufs, [vts[ch // 2][k0:k1] for ch in range(n_chains)]))
        return out

    def accumulate(tile, pieces, cs, accs):
        new_cs, new_accs = [[] for _ in range(n_chains)], [[] for _ in range(n_chains)]
        covered = 0
        for (r0, r1, _, _, mask), (zs, sufs, vts) in zip(pieces, tile):
            ws = [jnp.exp2(z - suf - c[r0:r1]) for z, suf, c in zip(zs, sufs, cs)]
            if mask is not None:
                ws = [jnp.where(mask, w, 0.0) for w in ws]
            for ch in range(n_chains):
                if r0 > covered:
                    new_accs[ch].append(jnp.zeros((r0 - covered, LANES), F32))
                    new_cs[ch].append(jnp.zeros((r0 - covered, 1), F32))
                new_accs[ch].append(_dot(ws[ch].astype(BF16), vts[ch]))
                new_cs[ch].append(sufs[ch][:, :1])
            covered = r1
        assert covered == TQ
        accs = tuple(acc + jnp.concatenate(parts, axis=0) for acc, parts in zip(accs, new_accs))
        cs = tuple(c + jnp.concatenate(parts, axis=0) for c, parts in zip(cs, new_cs))
        return cs, accs

    def c_min_of(cs):
        return functools.reduce(jnp.minimum, [jnp.min(c) for c in cs])

    def first_pass():
        diag = scores(kd_ref.at[0], vd_ref.at[0], diag_pieces)
        prev = scores(kp_ref.at[0], vp_ref.at[0], prev_pieces)
        cs, accs = accumulate(diag, diag_pieces, (jnp.zeros((TQ, 1), F32),) * n_chains,
                              (jnp.zeros((TQ, LANES), F32),) * n_chains)
        cs = tuple(jnp.where(i > 0, c, SB_NO_TILE) for c in cs)
        return accumulate(prev, prev_pieces, cs, accs)

    def write(accs):
        for p in range(HEAD_PAIRS_SB):
            o_ref[0, :, p * LANES:(p + 1) * LANES] = jnp.where(low, accs[2 * p], accs[2 * p + 1]).astype(BF16)

    cs, accs = first_pass()
    write(accs)

    @pl.when(jnp.logical_and(i > 0, c_min_of(cs) < SB_DONE))
    def _():
        def keep_going(carry):
            n, c_min, _, _ = carry
            return jnp.logical_and(n < i, c_min < SB_DONE)

        def step(carry):
            n, _, cs, accs = carry
            pieces = [(0, TQ, 0, TK, jnp.logical_or(n > 0, left_out))]
            keys = pl.ds(pl.multiple_of((i - 1 - n) * TK, TK), TK)
            copies = (pltpu.make_async_copy(k_hbm.at[pl.program_id(0), keys, :], k_buf, sem.at[0]),
                      pltpu.make_async_copy(v_hbm.at[pl.program_id(0), keys, :], v_buf, sem.at[1]))
            for copy in copies:
                copy.start()
            for copy in copies:
                copy.wait()
            cs, accs = accumulate(scores(k_buf, v_buf, pieces), pieces, cs, accs)
            return n + 1, c_min_of(cs), cs, accs

        cs, accs = first_pass()
        write(lax.while_loop(keep_going, step, (jnp.int32(0), c_min_of(cs), cs, accs))[3])


def _sb_attn(q, k, v):
    b, s, _ = q.shape
    qspec = pl.BlockSpec((1, TQ, SB_W), lambda bi, i: (bi, i, 0))
    before = pl.BlockSpec((1, TK, SB_W), lambda bi, i: (bi, jnp.maximum(i - 1, 0), 0))
    hbm = pl.BlockSpec(memory_space=pl.ANY)
    return pl.pallas_call(
        _sb_attn_kernel,
        grid=(b, s // TQ),
        in_specs=[qspec, qspec, before, qspec, before, hbm, hbm],
        out_specs=qspec,
        out_shape=jax.ShapeDtypeStruct((b, s, SB_W), BF16),
        scratch_shapes=[pltpu.VMEM((TK, SB_W), BF16), pltpu.VMEM((TK, SB_W), BF16),
                        pltpu.SemaphoreType.DMA((2,))],
        compiler_params=_params("parallel", "arbitrary"),
        name="sb_attn",
    )(q, k, k, v, v, k, v)


def kernel(x, mem, ffn1_norm, ffn1_w_gate, ffn1_w_up, ffn1_w_down, mix_norm, ffn2_norm, ffn2_w_gate, ffn2_w_up, ffn2_w_down, mem_norm, w_mem_kv, a_w_in, a_v_norm, a_w_spatial, a_b_spatial, a_w_out, kv_norm, w_kv, b_w_in, b_w_out, final_norm):
    b, s, d = x.shape
    assert d == D_MODEL and s % TM == 0 and s % TM_FFN == 0 and s % TQ == 0 and TQ == TK
    t = b * s
    xt = x.reshape(t, d)
    mem_kv = _mem_kv(mem.reshape(b * N_MEM, d), mem_norm, w_mem_kv)
    mem_kv = mem_kv.reshape(DEPTH * b, N_MEM, 2 * MEM_W)
    k = v = None
    for l in range(DEPTH):
        if l == N_A:
            k, v = _kv_proj(xt, kv_norm, w_kv)
            k = k.reshape(b, s, SB_W)
            v = v.reshape(b, s, SB_W)
        xt = _ffn(xt, ffn1_norm, ffn1_w_gate, ffn1_w_up, ffn1_w_down, l)
        if l < N_A:
            xt = _mixer_a(xt, mix_norm, a_w_in, a_v_norm, a_w_spatial, a_b_spatial, mem_kv, a_w_out, l, b)
        else:
            q_sb, q_mem = _b_in(xt, mix_norm, b_w_in, l, l - N_A)
            y_tok = _sb_attn(q_sb.reshape(b, s, SB_W), k, v).reshape(t, SB_W)
            xt = _b_out(xt, y_tok, q_mem, mem_kv, b_w_out, l, l - N_A, b)
        xt = _ffn(xt, ffn2_norm, ffn2_w_gate, ffn2_w_up, ffn2_w_down, l,
                  final_gain=final_norm if l == DEPTH - 1 else None)
    return xt.reshape(b, s, d)
```
